```python
import math
import jax, jax.numpy as jnp
from jax import lax
import numpy as np

D_MODEL = 4096
BATCH = 2
SEQ = 8192
DEPTH = 1

CHUNK = 64
N_META = 16
GLA_DK = 128
GLA_DV = 256
GLA_HEADS = D_MODEL // (2 * GLA_DK)
GLA_KEY_WIDTH = GLA_HEADS * GLA_DK
GLA_VAL_WIDTH = GLA_HEADS * GLA_DV
GATE_RANK = 16
GATE_TAU = 16.0
CONV_CH = D_MODEL
CONV_WIDTH = 31
N_BRANCH = 2
PEER_HEADS = 8
PEER_KEYS = 128
PEER_EXPERTS = PEER_KEYS * PEER_KEYS
PEER_QDIM = 256
PEER_HALF = PEER_QDIM // 2
PEER_TOPK = 16
PEER_BLOCK = 64
LN_EPS = 1e-5
RMS_EPS = 1e-6
DEEPNORM_ALPHA = (2.0 * DEPTH) ** 0.25
DEEPNORM_BETA = (8.0 * DEPTH) ** -0.25
OFF_Q = 0
OFF_K = OFF_Q + GLA_KEY_WIDTH
OFF_V = OFF_K + GLA_KEY_WIDTH
OFF_G = OFF_V + GLA_VAL_WIDTH
OFF_A = OFF_G + GLA_VAL_WIDTH
OFF_C = OFF_A + GATE_RANK
OFF_GATE = OFF_C + 2 * CONV_CH
IN_WIDTH = OFF_GATE + N_BRANCH * D_MODEL

kernel_name = 'hybrid_gla_conformer_peer_deepnorm'


def _layer_norm(x, g, b):
    xf = x.astype(jnp.float32)
    mu = jnp.mean(xf, axis=-1, keepdims=True)
    var = jnp.mean(jnp.square(xf - mu), axis=-1, keepdims=True)
    y = (xf - mu) * lax.rsqrt(var + LN_EPS)
    return (y * g.astype(jnp.float32) + b.astype(jnp.float32)).astype(x.dtype)


def _rms_norm(x, g):
    xf = x.astype(jnp.float32)
    y = xf * lax.rsqrt(jnp.mean(jnp.square(xf), axis=-1, keepdims=True) + RMS_EPS)
    return (y * g.astype(jnp.float32)).astype(x.dtype)


def _gla_chunk_causal(q, k, v, log_a):
    bsz, seqlen = q.shape[0], q.shape[1]
    pad = (-seqlen) % CHUNK
    n_chunks = (seqlen + pad) // CHUNK

    def to_chunks(t):
        t = jnp.pad(t.astype(jnp.float32), ((0, 0), (pad, 0), (0, 0), (0, 0)))
        t = t.reshape(bsz, n_chunks, CHUNK, t.shape[2], t.shape[3])
        return jnp.transpose(t, (1, 0, 3, 2, 4))

    qc, kc, vc, lac = (to_chunks(t) for t in (q, k, v, log_a))
    cum = jnp.cumsum(lac, axis=3)
    tot = cum[:, :, :, -1:, :]
    k_dec = kc * jnp.exp(tot - cum)
    chunk_decay = jnp.exp(tot[:, :, :, 0, :])

    def step(state, inp):
        q_i, k_i, v_i, d_i = inp
        state = d_i[..., None] * state + jnp.einsum('bhck,bhcv->bhkv', k_i, v_i)
        return state, jnp.einsum('bhck,bhkv->bhcv', q_i, state)

    s0 = jnp.zeros((bsz, GLA_HEADS, GLA_DK, GLA_DV), jnp.float32)
    _, o = lax.scan(step, s0, (qc, k_dec, vc, chunk_decay))
    o = jnp.transpose(o, (1, 0, 3, 2, 4)).reshape(bsz, n_chunks * CHUNK, GLA_HEADS, GLA_DV)
    return o[:, pad:]


def _causal_depthwise_conv(c, w, b):
    rhs = w.astype(c.dtype)[:, None, :]
    y = lax.conv_general_dilated(c, rhs, window_strides=(1,), padding=[(CONV_WIDTH - 1, 0)],
                                 dimension_numbers=('NWC', 'WIO', 'NWC'),
                                 feature_group_count=c.shape[-1])
    return y + b.astype(c.dtype)


def _mixer(h, w_in, w_a2, b_a, gla_norm_g, w_gla_o, conv_w, conv_b, conv_ln_g, conv_ln_b,
           w_conv_o, b_conv_o, w_out):
    bsz, seqlen, _ = h.shape

    def proj(off, size):
        return h @ w_in[:, off:off + size]

    q = proj(OFF_Q, GLA_KEY_WIDTH).reshape(bsz, seqlen, GLA_HEADS, GLA_DK) * (GLA_DK ** -0.5)
    k = proj(OFF_K, GLA_KEY_WIDTH).reshape(bsz, seqlen, GLA_HEADS, GLA_DK)
    v = proj(OFF_V, GLA_VAL_WIDTH).reshape(bsz, seqlen, GLA_HEADS, GLA_DV)
    g = proj(OFF_G, GLA_VAL_WIDTH)
    a_lr = proj(OFF_A, GATE_RANK)
    log_a = jax.nn.log_sigmoid((a_lr @ w_a2 + b_a).astype(jnp.float32)) / GATE_TAU
    log_a = log_a.reshape(bsz, seqlen, GLA_HEADS, GLA_DK)
    o = _gla_chunk_causal(q, k, v, log_a)
    o = _rms_norm(o, gla_norm_g).reshape(bsz, seqlen, GLA_VAL_WIDTH).astype(h.dtype)
    y_gla = (o * jax.nn.silu(g)) @ w_gla_o

    c = proj(OFF_C, CONV_CH) * jax.nn.sigmoid(proj(OFF_C + CONV_CH, CONV_CH))
    c = _causal_depthwise_conv(c, conv_w, conv_b)
    c = jax.nn.silu(_layer_norm(c, conv_ln_g, conv_ln_b))
    y_conv = c @ w_conv_o + b_conv_o

    gate_gla = jax.nn.sigmoid(proj(OFF_GATE, D_MODEL))
    gate_conv = jax.nn.sigmoid(proj(OFF_GATE + D_MODEL, D_MODEL))
    return (gate_gla * y_gla + gate_conv * y_conv) @ w_out


def _peer(h, wq, keys, u_tab, v_tab):
    bsz, seqlen, d = h.shape
    n_tok = bsz * seqlen
    xt = h.reshape(n_tok, d)
    q = (xt @ wq).reshape(n_tok, PEER_HEADS, 2, PEER_HALF)
    scores = jnp.einsum('tphd,phnd->tphn', q, keys).astype(jnp.float32)
    sub_val, sub_idx = lax.top_k(scores, PEER_TOPK)
    cand = (sub_val[:, :, 0, :, None] + sub_val[:, :, 1, None, :]).reshape(n_tok, PEER_HEADS, PEER_TOPK * PEER_TOPK)
    cand_idx = (sub_idx[:, :, 0, :, None] * PEER_KEYS + sub_idx[:, :, 1, None, :]).reshape(n_tok, PEER_HEADS, PEER_TOPK * PEER_TOPK)
    top_val, top_pos = lax.top_k(cand, PEER_TOPK)
    expert_idx = jnp.take_along_axis(cand_idx, top_pos, axis=-1).reshape(n_tok, PEER_HEADS * PEER_TOPK)
    gate = jax.nn.softmax(top_val, axis=-1).reshape(n_tok, PEER_HEADS * PEER_TOPK).astype(h.dtype)

    pad = (-n_tok) % PEER_BLOCK
    xt_p = jnp.pad(xt, ((0, pad), (0, 0))).reshape(-1, PEER_BLOCK, d)
    idx_p = jnp.pad(expert_idx, ((0, pad), (0, 0))).reshape(-1, PEER_BLOCK, PEER_HEADS * PEER_TOPK)
    gate_p = jnp.pad(gate, ((0, pad), (0, 0))).reshape(-1, PEER_BLOCK, PEER_HEADS * PEER_TOPK)

    def block(args):
        xb, ib, gb = args
        ub = jnp.take(u_tab, ib, axis=0)
        act = jax.nn.gelu(jnp.einsum('td,ted->te', xb, ub).astype(jnp.float32), approximate=False).astype(xb.dtype)
        vb = jnp.take(v_tab, ib, axis=0)
        return jnp.einsum('te,ted->td', gb * act, vb)

    out = lax.map(block, (xt_p, idx_p, gate_p)).reshape(-1, d)[:n_tok]
    return out.reshape(bsz, seqlen, d)


def setup_inputs(seed: int = 0) -> dict:
    key = jax.random.key(seed)
    ks = jax.random.split(key, 32)
    f32 = jnp.float32

    def nrm(k, shape, scale):
        return jax.random.normal(k, shape, f32) * scale

    beta = DEEPNORM_BETA
    col_scale = jnp.concatenate([
        jnp.ones((2 * GLA_KEY_WIDTH,), f32),
        jnp.full((GLA_VAL_WIDTH,), beta, f32),
        jnp.ones((IN_WIDTH - OFF_G,), f32)])
    return {
        'x': nrm(ks[0], (BATCH, SEQ, D_MODEL), 1.0),
        'meta': nrm(ks[1], (N_META, D_MODEL), 1.0),
        'ln0_g': 1.0 + nrm(ks[2], (D_MODEL,), 0.02),
        'ln0_b': nrm(ks[3], (D_MODEL,), 0.02),
        'w_in': nrm(ks[4], (DEPTH, D_MODEL, IN_WIDTH), D_MODEL ** -0.5) * col_scale,
        'w_a2': nrm(ks[5], (DEPTH, GATE_RANK, GLA_KEY_WIDTH), GATE_RANK ** -0.5),
        'b_a': nrm(ks[6], (DEPTH, GLA_KEY_WIDTH), 0.1),
        'gla_norm_g': 1.0 + nrm(ks[7], (DEPTH, GLA_DV), 0.02),
        'w_gla_o': nrm(ks[8], (DEPTH, GLA_VAL_WIDTH, D_MODEL), beta * GLA_VAL_WIDTH ** -0.5),
        'conv_w': nrm(ks[9], (DEPTH, CONV_WIDTH, CONV_CH), CONV_WIDTH ** -0.5),
        'conv_b': nrm(ks[10], (DEPTH, CONV_CH), 0.02),
        'conv_ln_g': 1.0 + nrm(ks[11], (DEPTH, CONV_CH), 0.02),
        'conv_ln_b': nrm(ks[12], (DEPTH, CONV_CH), 0.02),
        'w_conv_o': nrm(ks[13], (DEPTH, CONV_CH, D_MODEL), beta * CONV_CH ** -0.5),
        'b_conv_o': nrm(ks[14], (DEPTH, D_MODEL), 0.02),
        'w_out': nrm(ks[15], (DEPTH, D_MODEL, D_MODEL), beta * D_MODEL ** -0.5),
        'ln1_g': 1.0 + nrm(ks[16], (DEPTH, D_MODEL), 0.02),
        'ln1_b': nrm(ks[17], (DEPTH, D_MODEL), 0.02),
        'peer_wq': nrm(ks[18], (DEPTH, D_MODEL, PEER_HEADS * PEER_QDIM), D_MODEL ** -0.5),
        'peer_keys': nrm(ks[19], (DEPTH, PEER_HEADS, 2, PEER_KEYS, PEER_HALF), PEER_HALF ** -0.5),
        'peer_u': nrm(ks[20], (DEPTH, PEER_EXPERTS, D_MODEL), D_MODEL ** -0.5),
        'peer_v': nrm(ks[21], (DEPTH, PEER_EXPERTS, D_MODEL), beta * PEER_HEADS ** -0.5),
        'ln2_g': 1.0 + nrm(ks[22], (DEPTH, D_MODEL), 0.02),
        'ln2_b': nrm(ks[23], (DEPTH, D_MODEL), 0.02),
    }


def reference(x, meta, ln0_g, ln0_b, w_in, w_a2, b_a, gla_norm_g, w_gla_o, conv_w, conv_b,
              conv_ln_g, conv_ln_b, w_conv_o, b_conv_o, w_out, ln1_g, ln1_b, peer_wq, peer_keys,
              peer_u, peer_v, ln2_g, ln2_b):
    bsz = x.shape[0]
    meta_b = jnp.broadcast_to(meta[None].astype(x.dtype), (bsz, N_META, D_MODEL))
    h = _layer_norm(jnp.concatenate([meta_b, x], axis=1), ln0_g, ln0_b)
    for l in range(DEPTH):
        mix = _mixer(h, w_in[l], w_a2[l], b_a[l], gla_norm_g[l], w_gla_o[l], conv_w[l], conv_b[l],
                     conv_ln_g[l], conv_ln_b[l], w_conv_o[l], b_conv_o[l], w_out[l])
        h = _layer_norm(DEEPNORM_ALPHA * h + mix, ln1_g[l], ln1_b[l])
        ffn = _peer(h, peer_wq[l], peer_keys[l], peer_u[l], peer_v[l])
        h = _layer_norm(DEEPNORM_ALPHA * h + ffn, ln2_g[l], ln2_b[l])
    return h[:, N_META:]
```

```python
import functools
import math

import jax
import jax.numpy as jnp
from jax import lax
from jax.experimental import pallas as pl
from jax.experimental.pallas import tpu as pltpu

F32 = jnp.float32
BF16 = jnp.bfloat16

CHUNK = 64
GLA_DK = 128
GLA_DV = 256
GATE_TAU = 16.0
PEER_TOPK = 16
LN_EPS = 1e-5
RMS_EPS = 1e-6
ROW_ALIGN = 256
V7X_VMEM_BYTES = 64 * 1024 * 1024
VMEM_CAP = V7X_VMEM_BYTES - 8 * 1024 * 1024


def _pick(n, prefs):
    for p in prefs:
        if n % p == 0:
            return p
    return n


def _params(sem, vmem_bytes):
    return pltpu.CompilerParams(dimension_semantics=sem,
                                vmem_limit_bytes=int(min(max(vmem_bytes, 16 * 1024 * 1024), VMEM_CAP)))


def _ln_rows(x, g, b):
    mu = jnp.mean(x, axis=-1, keepdims=True)
    xc = x - mu
    var = jnp.mean(xc * xc, axis=-1, keepdims=True)
    return xc * lax.rsqrt(var + LN_EPS) * g + b


def _sigmoid(x):
    return 1.0 / (1.0 + jnp.exp(-x))


def _ln0_kernel(x_ref, meta_ref, g_ref, b_ref, hf_ref, hb_ref, *, n_meta, n_xc):
    i = pl.program_id(1)
    g = g_ref[...]
    b = b_ref[...]

    @pl.when(i == 0)
    def _():
        pad = CHUNK - n_meta
        y = _ln_rows(meta_ref[...], g, b)
        hf_ref[0, 0:pad, :] = jnp.zeros((pad, y.shape[1]), F32)
        hf_ref[0, pad:CHUNK, :] = y
        hb_ref[0, 0:pad, :] = jnp.zeros((pad, y.shape[1]), BF16)
        hb_ref[0, pad:CHUNK, :] = y.astype(BF16)

    @pl.when(jnp.logical_and(i >= 1, i <= n_xc))
    def _():
        y = _ln_rows(x_ref[0], g, b)
        hf_ref[0] = y
        hb_ref[0] = y.astype(BF16)

    @pl.when(i > n_xc)
    def _():
        hf_ref[...] = jnp.zeros_like(hf_ref)
        hb_ref[...] = jnp.zeros_like(hb_ref)


def _ln0(x, meta, g, b, lp):
    bsz, seq, d = x.shape
    n_meta = meta.shape[0]
    n_xc = seq // CHUNK
    kern = functools.partial(_ln0_kernel, n_meta=n_meta, n_xc=n_xc)
    hf, hb = pl.pallas_call(
        kern,
        grid=(bsz, lp // CHUNK),
        in_specs=[
            pl.BlockSpec((1, CHUNK, d), lambda bb, i: (bb, jnp.clip(i - 1, 0, n_xc - 1), 0)),
            pl.BlockSpec((n_meta, d), lambda bb, i: (0, 0)),
            pl.BlockSpec((1, d), lambda bb, i: (0, 0)),
            pl.BlockSpec((1, d), lambda bb, i: (0, 0)),
        ],
        out_specs=[
            pl.BlockSpec((1, CHUNK, d), lambda bb, i: (bb, i, 0)),
            pl.BlockSpec((1, CHUNK, d), lambda bb, i: (bb, i, 0)),
        ],
        out_shape=[jax.ShapeDtypeStruct((bsz, lp, d), F32),
                   jax.ShapeDtypeStruct((bsz, lp, d), BF16)],
        compiler_params=_params(("parallel", "arbitrary"), 32 * CHUNK * d),
        name="ln0",
    )(x, meta, g.reshape(1, d), b.reshape(1, d))
    return hf.reshape(bsz * lp, d), hb.reshape(bsz * lp, d)


def _mm_kernel(a_ref, b_ref, o_ref, *, act):
    acc = jnp.dot(a_ref[...], b_ref[...], preferred_element_type=F32)
    if act == "sigmoid":
        acc = _sigmoid(acc)
    o_ref[...] = acc.astype(o_ref.dtype)


def _matmul(a, b, out_dtype, *, act=None, name):
    m, k = a.shape
    n = b.shape[1]
    tm = _pick(m, (768, 512, 256, 128))
    tn = _pick(n, (1024, 512, 256, 128))
    osz = jnp.dtype(out_dtype).itemsize
    vmem = 2 * (tm * k * 2 + k * tn * 2 + tm * tn * osz) + 5 * tm * tn * 4
    return pl.pallas_call(
        functools.partial(_mm_kernel, act=act),
        grid=(m // tm, n // tn),
        in_specs=[pl.BlockSpec((tm, k), lambda i, j: (i, 0)),
                  pl.BlockSpec((k, tn), lambda i, j: (0, j))],
        out_specs=pl.BlockSpec((tm, tn), lambda i, j: (i, j)),
        out_shape=jax.ShapeDtypeStruct((m, n), out_dtype),
        compiler_params=_params(("parallel", "arbitrary"), vmem),
        name=name,
    )(a, b)


def _glu_kernel(a_ref, b1_ref, b2_ref, o_ref):
    a = a_ref[...]
    d1 = jnp.dot(a, b1_ref[...], preferred_element_type=F32)
    d2 = jnp.dot(a, b2_ref[...], preferred_element_type=F32)
    o_ref[...] = (d1 * _sigmoid(d2)).astype(o_ref.dtype)


def _glu_matmul(a, b12, n):
    m, k = a.shape
    tm = _pick(m, (768, 512, 256, 128))
    tn = _pick(n, (512, 256, 128))
    nb = n // tn
    vmem = 2 * (tm * k * 2 + 2 * k * tn * 2 + tm * tn * 2) + 8 * tm * tn * 4
    return pl.pallas_call(
        _glu_kernel,
        grid=(m // tm, nb),
        in_specs=[pl.BlockSpec((tm, k), lambda i, j: (i, 0)),
                  pl.BlockSpec((k, tn), lambda i, j: (0, j)),
                  pl.BlockSpec((k, tn), lambda i, j: (0, j + nb))],
        out_specs=pl.BlockSpec((tm, tn), lambda i, j: (i, j)),
        out_shape=jax.ShapeDtypeStruct((m, n), BF16),
        compiler_params=_params(("parallel", "arbitrary"), vmem),
        name="glu_proj",
    )(a, b12, b12)


def _gla_kernel(q_ref, k_ref, v_ref, g_ref, a_ref, wa2_ref, ba_ref, gn_ref, o_ref, st_ref, *, heads, rank):
    c = pl.program_id(1)

    @pl.when(c == 0)
    def _():
        st_ref[...] = jnp.zeros_like(st_ref)

    a_lr = a_ref[:, 0:rank]
    z = jnp.dot(a_lr, wa2_ref[...], preferred_element_type=F32,
                precision=lax.Precision.HIGHEST) + ba_ref[...]
    log_a = jax.nn.log_sigmoid(z) / GATE_TAU
    row = lax.broadcasted_iota(jnp.int32, (CHUNK, CHUNK), 0)
    col = lax.broadcasted_iota(jnp.int32, (CHUNK, CHUNK), 1)
    tri = (row >= col).astype(F32)
    cum = jnp.dot(tri, log_a, preferred_element_type=F32, precision=lax.Precision.HIGHEST)
    tot = cum[CHUNK - 1:CHUNK, :]
    k_scale = jnp.exp(tot - cum)
    decay = jnp.exp(tot)
    gn = gn_ref[...]
    q_scale = GLA_DK ** -0.5
    for h in range(heads):
        sk = slice(h * GLA_DK, (h + 1) * GLA_DK)
        sv = slice(h * GLA_DV, (h + 1) * GLA_DV)
        kd = (k_ref[:, sk].astype(F32) * k_scale[:, sk]).astype(BF16)
        upd = lax.dot_general(v_ref[:, sv], kd, (((0,), (0,)), ((), ())), preferred_element_type=F32)
        st = decay[:, sk] * st_ref[h] + upd
        st_ref[h] = st
        q = (q_ref[:, sk].astype(F32) * q_scale).astype(BF16)
        o = lax.dot_general(q, st.astype(BF16), (((1,), (1,)), ((), ())), preferred_element_type=F32)
        y = o * lax.rsqrt(jnp.mean(o * o, axis=-1, keepdims=True) + RMS_EPS) * gn
        gg = g_ref[:, sv].astype(F32)
        o_ref[:, sv] = (y * (gg * _sigmoid(gg))).astype(o_ref.dtype)


def _gla(qkvg, a_lr, w_a2, b_a, gn, bsz, lp):
    t, _ = qkvg.shape
    kw = w_a2.shape[1]
    heads = kw // GLA_DK
    vw = heads * GLA_DV
    rank = w_a2.shape[0]
    nc = lp // CHUNK
    aw = a_lr.shape[1]
    assert kw == vw // 2
    row = lambda bb, c: bb * nc + c
    vmem = 2 * (2 * CHUNK * kw * 2 + 3 * CHUNK * vw * 2) + heads * GLA_DK * GLA_DV * 4 + 12 * CHUNK * kw * 4
    return pl.pallas_call(
        functools.partial(_gla_kernel, heads=heads, rank=rank),
        grid=(bsz, nc),
        in_specs=[
            pl.BlockSpec((CHUNK, kw), lambda bb, c: (row(bb, c), 0)),
            pl.BlockSpec((CHUNK, kw), lambda bb, c: (row(bb, c), 1)),
            pl.BlockSpec((CHUNK, vw), lambda bb, c: (row(bb, c), 1)),
            pl.BlockSpec((CHUNK, vw), lambda bb, c: (row(bb, c), 2)),
            pl.BlockSpec((CHUNK, aw), lambda bb, c: (row(bb, c), 0)),
            pl.BlockSpec((rank, kw), lambda bb, c: (0, 0)),
            pl.BlockSpec((1, kw), lambda bb, c: (0, 0)),
            pl.BlockSpec((1, GLA_DV), lambda bb, c: (0, 0)),
        ],
        out_specs=pl.BlockSpec((CHUNK, vw), lambda bb, c: (row(bb, c), 0)),
        out_shape=jax.ShapeDtypeStruct((t, vw), BF16),
        scratch_shapes=[pltpu.VMEM((heads, GLA_DV, GLA_DK), F32)],
        compiler_params=_params(("parallel", "arbitrary"), vmem),
        name="gla",
    )(qkvg, qkvg, qkvg, qkvg, a_lr, w_a2, b_a.reshape(1, kw), gn.reshape(1, GLA_DV))


def _conv_kernel(c_ref, halo_ref, w_ref, b_ref, lg_ref, lb_ref, o_ref, buf_ref, *, tc, hb, width):
    i = pl.program_id(0)
    halo = halo_ref[...].astype(F32)
    buf_ref[0:hb, :] = jnp.where(i == 0, 0.0, halo)
    buf_ref[hb:hb + tc, :] = c_ref[...].astype(F32)
    acc = jnp.zeros((tc, c_ref.shape[1]), F32) + b_ref[...]
    for j in range(width):
        acc = acc + w_ref[j:j + 1, :] * buf_ref[pl.ds(hb - (width - 1) + j, tc), :]
    y = _ln_rows(acc, lg_ref[...], lb_ref[...])
    o_ref[...] = (y * _sigmoid(y)).astype(o_ref.dtype)


def _conv(c, w, b, lg, lb):
    t, d = c.shape
    width = w.shape[0]
    hb = 32
    assert width - 1 <= hb <= CHUNK
    tc = _pick(t, (256, 128))
    r = tc // hb
    vmem = 2 * (tc * d * 2 * 2 + hb * d * 2) + (tc + hb) * d * 4 + 6 * tc * d * 4
    return pl.pallas_call(
        functools.partial(_conv_kernel, tc=tc, hb=hb, width=width),
        grid=(t // tc,),
        in_specs=[
            pl.BlockSpec((tc, d), lambda i: (i, 0)),
            pl.BlockSpec((hb, d), lambda i: (jnp.maximum(i * r - 1, 0), 0)),
            pl.BlockSpec((width, d), lambda i: (0, 0)),
            pl.BlockSpec((1, d), lambda i: (0, 0)),
            pl.BlockSpec((1, d), lambda i: (0, 0)),
            pl.BlockSpec((1, d), lambda i: (0, 0)),
        ],
        out_specs=pl.BlockSpec((tc, d), lambda i: (i, 0)),
        out_shape=jax.ShapeDtypeStruct((t, d), BF16),
        scratch_shapes=[pltpu.VMEM((tc + hb, d), F32)],
        compiler_params=_params(("parallel",), vmem),
        name="conv_ln_silu",
    )(c, c, w, b.reshape(1, d), lg.reshape(1, d), lb.reshape(1, d))


def _merge_kernel(o_ref_in, c_ref, wg_ref, wc_ref, bc_ref, gg_ref, gc_ref, out_ref):
    yg = jnp.dot(o_ref_in[...], wg_ref[...], preferred_element_type=F32)
    yc = jnp.dot(c_ref[...], wc_ref[...], preferred_element_type=F32) + bc_ref[...]
    out = gg_ref[...].astype(F32) * yg + gc_ref[...].astype(F32) * yc
    out_ref[...] = out.astype(out_ref.dtype)


def _merge(o_gated, c_act, w_gla_o, w_conv_o, b_conv_o, gates):
    m, k1 = o_gated.shape
    k2 = c_act.shape[1]
    n = w_gla_o.shape[1]
    tm = _pick(m, (768, 512, 256, 128))
    tn = _pick(n, (512, 256, 128))
    nb = n // tn
    vmem = 2 * (tm * (k1 + k2) * 2 + (k1 + k2) * tn * 2 + 3 * tm * tn * 2) + 4 * tm * tn * 4
    return pl.pallas_call(
        _merge_kernel,
        grid=(m // tm, nb),
        in_specs=[
            pl.BlockSpec((tm, k1), lambda i, j: (i, 0)),
            pl.BlockSpec((tm, k2), lambda i, j: (i, 0)),
            pl.BlockSpec((k1, tn), lambda i, j: (0, j)),
            pl.BlockSpec((k2, tn), lambda i, j: (0, j)),
            pl.BlockSpec((1, tn), lambda i, j: (0, j)),
            pl.BlockSpec((tm, tn), lambda i, j: (i, j)),
            pl.BlockSpec((tm, tn), lambda i, j: (i, j + nb)),
        ],
        out_specs=pl.BlockSpec((tm, tn), lambda i, j: (i, j)),
        out_shape=jax.ShapeDtypeStruct((m, n), BF16),
        compiler_params=_params(("parallel", "arbitrary"), vmem),
        name="branch_merge",
    )(o_gated, c_act, w_gla_o, w_conv_o, b_conv_o.reshape(1, n), gates, gates)


def _resln_kernel(h_ref, y_ref, g_ref, b_ref, of_ref, *rest, alpha):
    y = _ln_rows(alpha * h_ref[...] + y_ref[...], g_ref[...], b_ref[...])
    of_ref[...] = y
    if rest:
        rest[0][...] = y.T.astype(BF16)


def _res_ln(h, y, g, b, alpha, *, transposed_copy):
    t, d = h.shape
    tr = _pick(t, (256, 128))
    out_specs = [pl.BlockSpec((tr, d), lambda i: (i, 0))]
    out_shape = [jax.ShapeDtypeStruct((t, d), F32)]
    if transposed_copy:
        out_specs.append(pl.BlockSpec((d, tr), lambda i: (0, i)))
        out_shape.append(jax.ShapeDtypeStruct((d, t), BF16))
    return pl.pallas_call(
        functools.partial(_resln_kernel, alpha=alpha),
        grid=(t // tr,),
        in_specs=[pl.BlockSpec((tr, d), lambda i: (i, 0)),
                  pl.BlockSpec((tr, d), lambda i: (i, 0)),
                  pl.BlockSpec((1, d), lambda i: (0, 0)),
                  pl.BlockSpec((1, d), lambda i: (0, 0))],
        out_specs=out_specs,
        out_shape=out_shape,
        compiler_params=_params(("parallel",), 16 * tr * d * 4),
        name="res_ln_t" if transposed_copy else "res_ln",
    )(h, y, g.reshape(1, d), b.reshape(1, d))


def _topk_rows(s, k):
    n = s.shape[0]
    row = lax.broadcasted_iota(jnp.int32, s.shape, 0)
    work = s
    outs = []
    for _ in range(k):
        m = jnp.max(work, axis=0, keepdims=True)
        outs.append(m)
        first = jnp.min(jnp.where(work == m, row, n), axis=0, keepdims=True)
        work = jnp.where(row == first, -jnp.inf, work)
    return jnp.concatenate(outs, axis=0)


def _peer_score_kernel(qt_ref, keys_ref, sc_ref, st_ref, *, heads, nk, half, topk):
    for h in range(heads):
        tops = []
        for p in range(2):
            idx = 2 * h + p
            s = jnp.dot(keys_ref[idx], qt_ref[idx * half:(idx + 1) * half, :], preferred_element_type=F32)
            sc_ref[idx * nk:(idx + 1) * nk, :] = s
            tops.append(_topk_rows(s, topk))
        t1, t2 = tops
        cand = jnp.concatenate([t1[a:a + 1, :] + t2 for a in range(topk)], axis=0)
        ct = _topk_rows(cand, topk)
        m1 = t1[0:1, :]
        m2 = t2[0:1, :]
        z = jnp.sum(jnp.exp(ct - (m1 + m2)), axis=0, keepdims=True)
        st_ref[h:h + 1, :] = ct[topk - 1:topk, :]
        st_ref[heads + h:heads + h + 1, :] = m1
        st_ref[2 * heads + h:2 * heads + h + 1, :] = m2
        st_ref[3 * heads + h:3 * heads + h + 1, :] = 1.0 / z


def _peer_scores(qt, keys):
    heads, _, nk, half = keys.shape
    t = qt.shape[1]
    tq = _pick(t, (256, 128))
    keys_f = keys.reshape(heads * 2, nk, half)
    return pl.pallas_call(
        functools.partial(_peer_score_kernel, heads=heads, nk=nk, half=half, topk=PEER_TOPK),
        grid=(t // tq,),
        in_specs=[pl.BlockSpec((heads * 2 * half, tq), lambda i: (0, i)),
                  pl.BlockSpec((heads * 2, nk, half), lambda i: (0, 0, 0))],
        out_specs=[pl.BlockSpec((heads * 2 * nk, tq), lambda i: (0, i)),
                   pl.BlockSpec((4 * heads, tq), lambda i: (0, i))],
        out_shape=[jax.ShapeDtypeStruct((heads * 2 * nk, t), F32),
                   jax.ShapeDtypeStruct((4 * heads, t), F32)],
        compiler_params=_params(("parallel",), 32 * 1024 * 1024),
        name="peer_scores",
    )(qt, keys_f)


def _peer_dense_kernel(xt_ref, u_ref, v_ref, sc_ref, st_ref, o_ref, e_ref, *, heads, nk, te):
    j = pl.program_id(1)

    @pl.when(j == 0)
    def _():
        o_ref[...] = jnp.zeros_like(o_ref)
        for h in range(heads):
            m1 = st_ref[heads + h:heads + h + 1, :]
            m2 = st_ref[2 * heads + h:2 * heads + h + 1, :]
            rz = st_ref[3 * heads + h:3 * heads + h + 1, :]
            r1 = slice((2 * h) * nk, (2 * h + 1) * nk)
            r2 = slice((2 * h + 1) * nk, (2 * h + 2) * nk)
            e_ref[r1, :] = jnp.exp(sc_ref[r1, :] - m1)
            e_ref[r2, :] = jnp.exp(sc_ref[r2, :] - m2) * rz

    act = jnp.dot(u_ref[...], xt_ref[...], preferred_element_type=F32)
    act = 0.5 * act * (1.0 + lax.erf(act * (2.0 ** -0.5)))
    parts = []
    for il in range(te // nk):
        i = j * (te // nk) + il
        w = None
        for h in range(heads):
            r2 = slice((2 * h + 1) * nk, (2 * h + 2) * nk)
            s1 = sc_ref[pl.ds((2 * h) * nk + i, 1), :]
            e1 = e_ref[pl.ds((2 * h) * nk + i, 1), :]
            sel = (s1 + sc_ref[r2, :]) >= st_ref[h:h + 1, :]
            wh = jnp.where(sel, e1 * e_ref[r2, :], 0.0)
            w = wh if w is None else w + wh
        parts.append(w)
    wt = jnp.concatenate(parts, axis=0) if len(parts) > 1 else parts[0]
    p = (wt * act).astype(BF16)
    o_ref[...] += lax.dot_general(p, v_ref[...], (((0,), (0,)), ((), ())), preferred_element_type=F32)


def _peer_dense(xt, u, v, sc, st, heads, nk):
    d, t = xt.shape
    ne = u.shape[0]
    tq = _pick(t, (512, 256, 128))
    te = _pick(ne, (256, 128))
    assert te % nk == 0 and ne == nk * nk
    vmem = (2 * (d * tq * 2 + 2 * te * d * 2 + 2 * heads * nk * tq * 4 + 4 * heads * tq * 4 + tq * d * 4)
            + 2 * heads * nk * tq * 4 + 8 * te * tq * 4)
    return pl.pallas_call(
        functools.partial(_peer_dense_kernel, heads=heads, nk=nk, te=te),
        grid=(t // tq, ne // te),
        in_specs=[pl.BlockSpec((d, tq), lambda i, j: (0, i)),
                  pl.BlockSpec((te, d), lambda i, j: (j, 0)),
                  pl.BlockSpec((te, d), lambda i, j: (j, 0)),
                  pl.BlockSpec((2 * heads * nk, tq), lambda i, j: (0, i)),
                  pl.BlockSpec((4 * heads, tq), lambda i, j: (0, i))],
        out_specs=pl.BlockSpec((tq, d), lambda i, j: (i, 0)),
        out_shape=jax.ShapeDtypeStruct((t, d), F32),
        scratch_shapes=[pltpu.VMEM((2 * heads * nk, tq), F32)],
        compiler_params=_params(("parallel", "arbitrary"), vmem),
        name="peer_dense",
    )(xt, u, v, sc, st)


def kernel(x, meta, ln0_g, ln0_b, w_in, w_a2, b_a, gla_norm_g, w_gla_o, conv_w, conv_b, conv_ln_g, conv_ln_b, w_conv_o, b_conv_o, w_out, ln1_g, ln1_b, peer_wq, peer_keys, peer_u, peer_v, ln2_g, ln2_b):
    bsz, seq, d = x.shape
    n_meta = meta.shape[0]
    depth = w_in.shape[0]
    kw = w_a2.shape[2]
    vw = w_gla_o.shape[1]
    rank = w_a2.shape[1]
    cc = conv_w.shape[2]
    assert seq % CHUNK == 0 and 0 < n_meta <= CHUNK
    lp = -(-(CHUNK + seq) // ROW_ALIGN) * ROW_ALIGN
    alpha = (2.0 * depth) ** 0.25
    off_a = 2 * kw + 2 * vw
    off_c = off_a + rank
    off_gate = off_c + 2 * cc

    hf, hb = _ln0(x, meta, ln0_g, ln0_b, lp)
    for l in range(depth):
        wl = w_in[l]
        w_qkvg = wl[:, :off_a].astype(BF16)
        w_a = jnp.pad(wl[:, off_a:off_c], ((0, 0), (0, 128 - rank))).astype(BF16)
        w_c = wl[:, off_c:off_gate].astype(BF16)
        w_gate = wl[:, off_gate:].astype(BF16)

        qkvg = _matmul(hb, w_qkvg, BF16, name="proj_qkvg")
        a_lr = _matmul(hb, w_a, F32, name="proj_gate_lowrank")
        o_gated = _gla(qkvg, a_lr, w_a2[l], b_a[l], gla_norm_g[l], bsz, lp)

        c = _glu_matmul(hb, w_c, cc)
        c_act = _conv(c, conv_w[l], conv_b[l], conv_ln_g[l], conv_ln_b[l])

        gates = _matmul(hb, w_gate, BF16, act="sigmoid", name="proj_branch_gates")
        merged = _merge(o_gated, c_act, w_gla_o[l].astype(BF16), w_conv_o[l].astype(BF16), b_conv_o[l], gates)
        mix = _matmul(merged, w_out[l].astype(BF16), F32, name="proj_out")
        h1, h1t = _res_ln(hf, mix, ln1_g[l], ln1_b[l], alpha, transposed_copy=True)

        heads, _, nk, _ = peer_keys[l].shape
        qt = _matmul(peer_wq[l].T.astype(BF16), h1t, BF16, name="peer_query")
        sc, st = _peer_scores(qt, peer_keys[l].astype(BF16))
        ffn = _peer_dense(h1t, peer_u[l].astype(BF16), peer_v[l].astype(BF16), sc, st, heads, nk)
        (hf,) = _res_ln(h1, ffn, ln2_g[l], ln2_b[l], alpha, transposed_copy=False)
        if l + 1 < depth:
            hb = hf.astype(BF16)
    return hf.reshape(bsz, lp, d)[:, CHUNK:CHUNK + seq]
```

```python
import functools
import math

import jax
import jax.numpy as jnp
from jax import lax
from jax.experimental import pallas as pl
from jax.experimental.pallas import tpu as pltpu

F32 = jnp.float32
BF16 = jnp.bfloat16

CHUNK = 64
GLA_DK = 128
GLA_DV = 256
GATE_TAU = 16.0
PEER_TOPK = 16
LN_EPS = 1e-5
RMS_EPS = 1e-6
SUBLANES = 8
LANES = 128
CONV_ROWS = 128
ROW_ALIGN = 256
V7X_VMEM_BYTES = 64 * 1024 * 1024
VMEM_CAP = V7X_VMEM_BYTES - 8 * 1024 * 1024


def _pick(n, prefs):
    for p in prefs:
        if n % p == 0:
            return p
    return n


def _params(sem, vmem_bytes):
    return pltpu.CompilerParams(dimension_semantics=sem,
                                vmem_limit_bytes=int(min(max(vmem_bytes, 16 * 1024 * 1024), VMEM_CAP)))


def _ln_rows(x, g, b):
    mu = jnp.mean(x, axis=-1, keepdims=True)
    xc = x - mu
    var = jnp.mean(xc * xc, axis=-1, keepdims=True)
    return xc * lax.rsqrt(var + LN_EPS) * g + b


def _sigmoid(x):
    return 1.0 / (1.0 + jnp.exp(-x))


def _ln0_kernel(x_ref, meta_ref, g_ref, b_ref, hf_ref, hb_ref, *, n_meta, n_xc):
    i = pl.program_id(1)
    g = g_ref[...]
    b = b_ref[...]

    @pl.when(i == 0)
    def _():
        pad = CHUNK - n_meta
        y = _ln_rows(meta_ref[...], g, b)
        hf_ref[0, 0:pad, :] = jnp.zeros((pad, y.shape[1]), F32)
        hf_ref[0, pad:CHUNK, :] = y
        hb_ref[0, 0:pad, :] = jnp.zeros((pad, y.shape[1]), BF16)
        hb_ref[0, pad:CHUNK, :] = y.astype(BF16)

    @pl.when(jnp.logical_and(i >= 1, i <= n_xc))
    def _():
        y = _ln_rows(x_ref[0], g, b)
        hf_ref[0] = y
        hb_ref[0] = y.astype(BF16)

    @pl.when(i > n_xc)
    def _():
        hf_ref[...] = jnp.zeros_like(hf_ref)
        hb_ref[...] = jnp.zeros_like(hb_ref)


def _ln0(x, meta, g, b, lp):
    bsz, seq, d = x.shape
    n_meta = meta.shape[0]
    n_xc = seq // CHUNK
    kern = functools.partial(_ln0_kernel, n_meta=n_meta, n_xc=n_xc)
    hf, hb = pl.pallas_call(
        kern,
        grid=(bsz, lp // CHUNK),
        in_specs=[
            pl.BlockSpec((1, CHUNK, d), lambda bb, i: (bb, jnp.clip(i - 1, 0, n_xc - 1), 0)),
            pl.BlockSpec((n_meta, d), lambda bb, i: (0, 0)),
            pl.BlockSpec((1, d), lambda bb, i: (0, 0)),
            pl.BlockSpec((1, d), lambda bb, i: (0, 0)),
        ],
        out_specs=[
            pl.BlockSpec((1, CHUNK, d), lambda bb, i: (bb, i, 0)),
            pl.BlockSpec((1, CHUNK, d), lambda bb, i: (bb, i, 0)),
        ],
        out_shape=[jax.ShapeDtypeStruct((bsz, lp, d), F32),
                   jax.ShapeDtypeStruct((bsz, lp, d), BF16)],
        compiler_params=_params(("parallel", "arbitrary"), 32 * CHUNK * d),
        name="ln0",
    )(x, meta, g.reshape(1, d), b.reshape(1, d))
    return hf.reshape(bsz * lp, d), hb.reshape(bsz * lp, d)


def _mm_kernel(a_ref, b_ref, o_ref, *, act):
    acc = jnp.dot(a_ref[...], b_ref[...], preferred_element_type=F32)
    if act == "sigmoid":
        acc = _sigmoid(acc)
    o_ref[...] = acc.astype(o_ref.dtype)


def _matmul(a, b, out_dtype, *, act=None, name):
    m, k = a.shape
    n = b.shape[1]
    tm = _pick(m, (768, 512, 256, 128))
    tn = _pick(n, (1024, 512, 256, 128))
    osz = jnp.dtype(out_dtype).itemsize
    vmem = 2 * (tm * k * 2 + k * tn * 2 + tm * tn * osz) + 5 * tm * tn * 4
    return pl.pallas_call(
        functools.partial(_mm_kernel, act=act),
        grid=(m // tm, n // tn),
        in_specs=[pl.BlockSpec((tm, k), lambda i, j: (i, 0)),
                  pl.BlockSpec((k, tn), lambda i, j: (0, j))],
        out_specs=pl.BlockSpec((tm, tn), lambda i, j: (i, j)),
        out_shape=jax.ShapeDtypeStruct((m, n), out_dtype),
        compiler_params=_params(("parallel", "arbitrary"), vmem),
        name=name,
    )(a, b)


def _glu_kernel(a_ref, b1_ref, b2_ref, o_ref):
    a = a_ref[...]
    d1 = jnp.dot(a, b1_ref[...], preferred_element_type=F32)
    d2 = jnp.dot(a, b2_ref[...], preferred_element_type=F32)
    o_ref[...] = (d1 * _sigmoid(d2)).astype(o_ref.dtype)


def _glu_matmul(a, b12, n):
    m, k = a.shape
    tm = _pick(m, (768, 512, 256, 128))
    tn = _pick(n, (512, 256, 128))
    nb = n // tn
    vmem = 2 * (tm * k * 2 + 2 * k * tn * 2 + tm * tn * 2) + 8 * tm * tn * 4
    return pl.pallas_call(
        _glu_kernel,
        grid=(m // tm, nb),
        in_specs=[pl.BlockSpec((tm, k), lambda i, j: (i, 0)),
                  pl.BlockSpec((k, tn), lambda i, j: (0, j)),
                  pl.BlockSpec((k, tn), lambda i, j: (0, j + nb))],
        out_specs=pl.BlockSpec((tm, tn), lambda i, j: (i, j)),
        out_shape=jax.ShapeDtypeStruct((m, n), BF16),
        compiler_params=_params(("parallel", "arbitrary"), vmem),
        name="glu_proj",
    )(a, b12, b12)


def _gla_kernel(q_ref, k_ref, v_ref, g_ref, a_ref, wa2_ref, ba_ref, gn_ref, o_ref, st_ref, *, heads, rank):
    c = pl.program_id(1)

    @pl.when(c == 0)
    def _():
        st_ref[...] = jnp.zeros_like(st_ref)

    a_lr = a_ref[:, 0:rank]
    z = jnp.dot(a_lr, wa2_ref[...], preferred_element_type=F32,
                precision=lax.Precision.HIGHEST) + ba_ref[...]
    log_a = jax.nn.log_sigmoid(z) / GATE_TAU
    row = lax.broadcasted_iota(jnp.int32, (CHUNK, CHUNK), 0)
    col = lax.broadcasted_iota(jnp.int32, (CHUNK, CHUNK), 1)
    tri = (row >= col).astype(F32)
    cum = jnp.dot(tri, log_a, preferred_element_type=F32, precision=lax.Precision.HIGHEST)
    tot = cum[CHUNK - 1:CHUNK, :]
    k_scale = jnp.exp(tot - cum)
    decay = jnp.exp(tot)
    gn = gn_ref[...]
    q_scale = GLA_DK ** -0.5
    for h in range(heads):
        sk = slice(h * GLA_DK, (h + 1) * GLA_DK)
        sv = slice(h * GLA_DV, (h + 1) * GLA_DV)
        kd = (k_ref[:, sk].astype(F32) * k_scale[:, sk]).astype(BF16)
        upd = lax.dot_general(v_ref[:, sv], kd, (((0,), (0,)), ((), ())), preferred_element_type=F32)
        st = decay[:, sk] * st_ref[h] + upd
        st_ref[h] = st
        q = (q_ref[:, sk].astype(F32) * q_scale).astype(BF16)
        o = lax.dot_general(q, st.astype(BF16), (((1,), (1,)), ((), ())), preferred_element_type=F32)
        y = o * lax.rsqrt(jnp.mean(o * o, axis=-1, keepdims=True) + RMS_EPS) * gn
        gg = g_ref[:, sv].astype(F32)
        o_ref[:, sv] = (y * (gg * _sigmoid(gg))).astype(o_ref.dtype)


def _gla(qkvg, a_lr, w_a2, b_a, gn, bsz, lp):
    t, _ = qkvg.shape
    kw = w_a2.shape[1]
    heads = kw // GLA_DK
    vw = heads * GLA_DV
    rank = w_a2.shape[0]
    nc = lp // CHUNK
    aw = a_lr.shape[1]
    assert kw == vw // 2
    row = lambda bb, c: bb * nc + c
    vmem = 2 * (2 * CHUNK * kw * 2 + 3 * CHUNK * vw * 2) + heads * GLA_DK * GLA_DV * 4 + 12 * CHUNK * kw * 4
    return pl.pallas_call(
        functools.partial(_gla_kernel, heads=heads, rank=rank),
        grid=(bsz, nc),
        in_specs=[
            pl.BlockSpec((CHUNK, kw), lambda bb, c: (row(bb, c), 0)),
            pl.BlockSpec((CHUNK, kw), lambda bb, c: (row(bb, c), 1)),
            pl.BlockSpec((CHUNK, vw), lambda bb, c: (row(bb, c), 1)),
            pl.BlockSpec((CHUNK, vw), lambda bb, c: (row(bb, c), 2)),
            pl.BlockSpec((CHUNK, aw), lambda bb, c: (row(bb, c), 0)),
            pl.BlockSpec((rank, kw), lambda bb, c: (0, 0)),
            pl.BlockSpec((1, kw), lambda bb, c: (0, 0)),
            pl.BlockSpec((1, GLA_DV), lambda bb, c: (0, 0)),
        ],
        out_specs=pl.BlockSpec((CHUNK, vw), lambda bb, c: (row(bb, c), 0)),
        out_shape=jax.ShapeDtypeStruct((t, vw), BF16),
        scratch_shapes=[pltpu.VMEM((heads, GLA_DV, GLA_DK), F32)],
        compiler_params=_params(("parallel", "arbitrary"), vmem),
        name="gla",
    )(qkvg, qkvg, qkvg, qkvg, a_lr, w_a2, b_a.reshape(1, kw), gn.reshape(1, GLA_DV))


def _conv_kernel(c_ref, halo_ref, w_ref, b_ref, lg_ref, lb_ref, o_ref, buf_ref, y_ref, *, tc, hb, width):
    i = pl.program_id(0)
    d = c_ref.shape[1]
    halo = halo_ref[...].astype(F32)
    buf_ref[0:hb, :] = jnp.where(i == 0, 0.0, halo)
    buf_ref[hb:hb + tc, :] = c_ref[...].astype(F32)
    buf_ref[hb + tc:hb + tc + SUBLANES, :] = jnp.zeros((SUBLANES, d), F32)
    base = hb - (width - 1)
    rows = CONV_ROWS
    n_rc = tc // rows

    def chunk(t, carry):
        r0 = pl.multiple_of((t % n_rc) * rows, rows)
        l0 = pl.multiple_of((t // n_rc) * LANES, LANES)
        acc = jnp.zeros((rows, LANES), F32) + b_ref[:, pl.ds(l0, LANES)]
        for s in range(SUBLANES):
            part = None
            for j in range(width):
                if (base + j) % SUBLANES != s:
                    continue
                a0 = ((base + j) // SUBLANES) * SUBLANES
                term = w_ref[j:j + 1, pl.ds(l0, LANES)] * buf_ref[pl.ds(r0 + a0, rows + SUBLANES), pl.ds(l0, LANES)]
                part = term if part is None else part + term
            if part is not None:
                acc = acc + part[s:s + rows, :]
        y_ref[pl.ds(r0, rows), pl.ds(l0, LANES)] = acc
        return carry

    lax.fori_loop(0, n_rc * (d // LANES), chunk, 0)
    y = _ln_rows(y_ref[...], lg_ref[...], lb_ref[...])
    o_ref[...] = (y * _sigmoid(y)).astype(o_ref.dtype)


def _conv(c, w, b, lg, lb):
    t, d = c.shape
    width = w.shape[0]
    hb = 32
    tc = _pick(t, (256, 128))
    assert width - 1 <= hb <= CHUNK and tc % CONV_ROWS == 0 and d % LANES == 0
    r = tc // hb
    vmem = 2 * (tc * d * 2 * 2 + hb * d * 2) + (2 * tc + hb + SUBLANES) * d * 4 + 6 * tc * d * 4
    return pl.pallas_call(
        functools.partial(_conv_kernel, tc=tc, hb=hb, width=width),
        grid=(t // tc,),
        in_specs=[
            pl.BlockSpec((tc, d), lambda i: (i, 0)),
            pl.BlockSpec((hb, d), lambda i: (jnp.maximum(i * r - 1, 0), 0)),
            pl.BlockSpec((width, d), lambda i: (0, 0)),
            pl.BlockSpec((1, d), lambda i: (0, 0)),
            pl.BlockSpec((1, d), lambda i: (0, 0)),
            pl.BlockSpec((1, d), lambda i: (0, 0)),
        ],
        out_specs=pl.BlockSpec((tc, d), lambda i: (i, 0)),
        out_shape=jax.ShapeDtypeStruct((t, d), BF16),
        scratch_shapes=[pltpu.VMEM((tc + hb + SUBLANES, d), F32), pltpu.VMEM((tc, d), F32)],
        compiler_params=_params(("parallel",), vmem),
        name="conv_ln_silu",
    )(c, c, w, b.reshape(1, d), lg.reshape(1, d), lb.reshape(1, d))


def _merge_kernel(o_ref_in, c_ref, wg_ref, wc_ref, bc_ref, gg_ref, gc_ref, out_ref):
    yg = jnp.dot(o_ref_in[...], wg_ref[...], preferred_element_type=F32)
    yc = jnp.dot(c_ref[...], wc_ref[...], preferred_element_type=F32) + bc_ref[...]
    out = gg_ref[...].astype(F32) * yg + gc_ref[...].astype(F32) * yc
    out_ref[...] = out.astype(out_ref.dtype)


def _merge(o_gated, c_act, w_gla_o, w_conv_o, b_conv_o, gates):
    m, k1 = o_gated.shape
    k2 = c_act.shape[1]
    n = w_gla_o.shape[1]
    tm = _pick(m, (768, 512, 256, 128))
    tn = _pick(n, (512, 256, 128))
    nb = n // tn
    vmem = 2 * (tm * (k1 + k2) * 2 + (k1 + k2) * tn * 2 + 3 * tm * tn * 2) + 4 * tm * tn * 4
    return pl.pallas_call(
        _merge_kernel,
        grid=(m // tm, nb),
        in_specs=[
            pl.BlockSpec((tm, k1), lambda i, j: (i, 0)),
            pl.BlockSpec((tm, k2), lambda i, j: (i, 0)),
            pl.BlockSpec((k1, tn), lambda i, j: (0, j)),
            pl.BlockSpec((k2, tn), lambda i, j: (0, j)),
            pl.BlockSpec((1, tn), lambda i, j: (0, j)),
            pl.BlockSpec((tm, tn), lambda i, j: (i, j)),
            pl.BlockSpec((tm, tn), lambda i, j: (i, j + nb)),
        ],
        out_specs=pl.BlockSpec((tm, tn), lambda i, j: (i, j)),
        out_shape=jax.ShapeDtypeStruct((m, n), BF16),
        compiler_params=_params(("parallel", "arbitrary"), vmem),
        name="branch_merge",
    )(o_gated, c_act, w_gla_o, w_conv_o, b_conv_o.reshape(1, n), gates, gates)


def _resln_kernel(h_ref, y_ref, g_ref, b_ref, of_ref, *rest, alpha):
    y = _ln_rows(alpha * h_ref[...] + y_ref[...], g_ref[...], b_ref[...])
    of_ref[...] = y
    if rest:
        rest[0][...] = y.T.astype(BF16)


def _res_ln(h, y, g, b, alpha, *, transposed_copy):
    t, d = h.shape
    tr = _pick(t, (256, 128))
    out_specs = [pl.BlockSpec((tr, d), lambda i: (i, 0))]
    out_shape = [jax.ShapeDtypeStruct((t, d), F32)]
    if transposed_copy:
        out_specs.append(pl.BlockSpec((d, tr), lambda i: (0, i)))
        out_shape.append(jax.ShapeDtypeStruct((d, t), BF16))
    return pl.pallas_call(
        functools.partial(_resln_kernel, alpha=alpha),
        grid=(t // tr,),
        in_specs=[pl.BlockSpec((tr, d), lambda i: (i, 0)),
                  pl.BlockSpec((tr, d), lambda i: (i, 0)),
                  pl.BlockSpec((1, d), lambda i: (0, 0)),
                  pl.BlockSpec((1, d), lambda i: (0, 0))],
        out_specs=out_specs,
        out_shape=out_shape,
        compiler_params=_params(("parallel",), 16 * tr * d * 4),
        name="res_ln_t" if transposed_copy else "res_ln",
    )(h, y, g.reshape(1, d), b.reshape(1, d))


def _final_ln_kernel(h_ref, y_ref, g_ref, b_ref, o_ref, *, alpha):
    o_ref[0] = _ln_rows(alpha * h_ref[...] + y_ref[...], g_ref[...], b_ref[...])


def _final_res_ln(h, y, g, b, alpha, bsz, lp, seq):
    d = h.shape[1]
    nc = lp // CHUNK
    row = lambda bb, c: (bb * nc + c + 1, 0)
    return pl.pallas_call(
        functools.partial(_final_ln_kernel, alpha=alpha),
        grid=(bsz, seq // CHUNK),
        in_specs=[pl.BlockSpec((CHUNK, d), row),
                  pl.BlockSpec((CHUNK, d), row),
                  pl.BlockSpec((1, d), lambda bb, c: (0, 0)),
                  pl.BlockSpec((1, d), lambda bb, c: (0, 0))],
        out_specs=pl.BlockSpec((1, CHUNK, d), lambda bb, c: (bb, c, 0)),
        out_shape=jax.ShapeDtypeStruct((bsz, seq, d), F32),
        compiler_params=_params(("parallel", "parallel"), 16 * CHUNK * d * 4),
        name="final_res_ln",
    )(h, y, g.reshape(1, d), b.reshape(1, d))


def _sort_desc(v):
    n = len(v)
    k = 2
    while k <= n:
        j = k // 2
        while j >= 1:
            for i in range(n):
                l = i ^ j
                if l > i:
                    hi, lo = jnp.maximum(v[i], v[l]), jnp.minimum(v[i], v[l])
                    v[i], v[l] = (hi, lo) if (i & k) == 0 else (lo, hi)
            j //= 2
        k *= 2
    return v


def _merge_desc(v):
    n = len(v)
    j = n // 2
    while j >= 1:
        for i in range(n):
            l = i ^ j
            if l > i:
                v[i], v[l] = jnp.maximum(v[i], v[l]), jnp.minimum(v[i], v[l])
        j //= 2
    return v


def _merge_top(a, b):
    n = len(a)
    return _merge_desc([jnp.maximum(a[i], b[n - 1 - i]) for i in range(n)])


def _pad_pow2(v, n, fill):
    m = max(n, 1 << (len(v) - 1).bit_length())
    return v + [fill] * (m - len(v))


def _peer_score_kernel(qt_ref, keys_ref, sc_ref, st_ref, *, heads, nk, half, topk):
    tq = qt_ref.shape[1]
    for idx in range(2 * heads):
        sc_ref[idx * nk:(idx + 1) * nk, :] = jnp.dot(
            keys_ref[idx], qt_ref[idx * half:(idx + 1) * half, :], preferred_element_type=F32)
    sub = lax.broadcasted_iota(jnp.int32, (SUBLANES, LANES), 0)
    ninf = jnp.full((SUBLANES, LANES), -jnp.inf, F32)
    for lt in range(tq // LANES):
        ls = slice(lt * LANES, (lt + 1) * LANES)
        packed = []
        for p in range(2):
            pk = [ninf] * topk
            for h in range(heads):
                r0 = (2 * h + p) * nk
                v = [sc_ref[r0 + SUBLANES * g:r0 + SUBLANES * (g + 1), ls] for g in range(nk // SUBLANES)]
                v = _sort_desc(_pad_pow2(v, topk, ninf))[:topk]
                shift = SUBLANES // 2
                while shift >= 1:
                    v = _merge_top(v, [pltpu.roll(x, shift, 0) for x in v])
                    shift //= 2
                pk = [jnp.where(sub == h, v[a], pk[a]) for a in range(topk)]
            packed.append(pk)
        t1, t2 = packed
        lists = [[t1[a] + t2[b] for b in range(topk // (a + 1))] for a in range(topk)]
        best = lists[0]
        rest = [x for l in lists[2:] for x in l]
        if len(lists) > 1:
            best = _merge_top(best, lists[1] + [ninf] * (topk - len(lists[1])))
        if rest:
            best = _merge_top(best, _sort_desc(_pad_pow2(rest, topk, ninf))[:topk])
        m = t1[0] + t2[0]
        z = jnp.exp(best[0] - m)
        for a in range(1, topk):
            z = z + jnp.exp(best[a] - m)
        st_ref[0 * SUBLANES:1 * SUBLANES, ls] = best[topk - 1]
        st_ref[1 * SUBLANES:2 * SUBLANES, ls] = t1[0]
        st_ref[2 * SUBLANES:3 * SUBLANES, ls] = t2[0]
        st_ref[3 * SUBLANES:4 * SUBLANES, ls] = 1.0 / z


def _peer_scores(qt, keys):
    heads, _, nk, half = keys.shape
    t = qt.shape[1]
    tq = _pick(t, (256, 128))
    keys_f = keys.reshape(heads * 2, nk, half)
    assert heads <= SUBLANES and nk % SUBLANES == 0 and (PEER_TOPK & (PEER_TOPK - 1)) == 0
    return pl.pallas_call(
        functools.partial(_peer_score_kernel, heads=heads, nk=nk, half=half, topk=PEER_TOPK),
        grid=(t // tq,),
        in_specs=[pl.BlockSpec((heads * 2 * half, tq), lambda i: (0, i)),
                  pl.BlockSpec((heads * 2, nk, half), lambda i: (0, 0, 0))],
        out_specs=[pl.BlockSpec((heads * 2 * nk, tq), lambda i: (0, i)),
                   pl.BlockSpec((4 * SUBLANES, tq), lambda i: (0, i))],
        out_shape=[jax.ShapeDtypeStruct((heads * 2 * nk, t), F32),
                   jax.ShapeDtypeStruct((4 * SUBLANES, t), F32)],
        compiler_params=_params(("parallel",), 32 * 1024 * 1024),
        name="peer_scores",
    )(qt, keys_f)


def _peer_dense_kernel(xt_ref, u_ref, v_ref, sc_ref, st_ref, o_ref, e_ref, *, heads, nk, te):
    j = pl.program_id(1)

    @pl.when(j == 0)
    def _():
        o_ref[...] = jnp.zeros_like(o_ref)
        for h in range(heads):
            m1 = st_ref[SUBLANES + h:SUBLANES + h + 1, :]
            m2 = st_ref[2 * SUBLANES + h:2 * SUBLANES + h + 1, :]
            rz = st_ref[3 * SUBLANES + h:3 * SUBLANES + h + 1, :]
            r1 = slice((2 * h) * nk, (2 * h + 1) * nk)
            r2 = slice((2 * h + 1) * nk, (2 * h + 2) * nk)
            e_ref[r1, :] = jnp.exp(sc_ref[r1, :] - m1)
            e_ref[r2, :] = jnp.exp(sc_ref[r2, :] - m2) * rz

    act = jnp.dot(u_ref[...], xt_ref[...], preferred_element_type=F32)
    act = 0.5 * act * (1.0 + lax.erf(act * (2.0 ** -0.5)))
    parts = []
    for il in range(te // nk):
        i = j * (te // nk) + il
        w = None
        for h in range(heads):
            r2 = slice((2 * h + 1) * nk, (2 * h + 2) * nk)
            s1 = sc_ref[pl.ds((2 * h) * nk + i, 1), :]
            e1 = e_ref[pl.ds((2 * h) * nk + i, 1), :]
            wh = e1 * jnp.where((s1 + sc_ref[r2, :]) >= st_ref[h:h + 1, :], e_ref[r2, :], 0.0)
            w = wh if w is None else w + wh
        parts.append(w)
    wt = jnp.concatenate(parts, axis=0) if len(parts) > 1 else parts[0]
    p = (wt * act).astype(BF16)
    o_ref[...] += lax.dot_general(p, v_ref[...], (((0,), (0,)), ((), ())), preferred_element_type=F32)


def _peer_dense(xt, u, v, sc, st, heads, nk):
    d, t = xt.shape
    ne = u.shape[0]
    tq = _pick(t, (512, 256, 128))
    te = _pick(ne, (256, 128))
    assert te % nk == 0 and ne == nk * nk
    vmem = (2 * (d * tq * 2 + 2 * te * d * 2 + 2 * heads * nk * tq * 4 + 4 * SUBLANES * tq * 4 + tq * d * 4)
            + 2 * heads * nk * tq * 4 + 8 * te * tq * 4)
    return pl.pallas_call(
        functools.partial(_peer_dense_kernel, heads=heads, nk=nk, te=te),
        grid=(t // tq, ne // te),
        in_specs=[pl.BlockSpec((d, tq), lambda i, j: (0, i)),
                  pl.BlockSpec((te, d), lambda i, j: (j, 0)),
                  pl.BlockSpec((te, d), lambda i, j: (j, 0)),
                  pl.BlockSpec((2 * heads * nk, tq), lambda i, j: (0, i)),
                  pl.BlockSpec((4 * SUBLANES, tq), lambda i, j: (0, i))],
        out_specs=pl.BlockSpec((tq, d), lambda i, j: (i, 0)),
        out_shape=jax.ShapeDtypeStruct((t, d), F32),
        scratch_shapes=[pltpu.VMEM((2 * heads * nk, tq), F32)],
        compiler_params=_params(("parallel", "arbitrary"), vmem),
        name="peer_dense",
    )(xt, u, v, sc, st)


def kernel(x, meta, ln0_g, ln0_b, w_in, w_a2, b_a, gla_norm_g, w_gla_o, conv_w, conv_b, conv_ln_g, conv_ln_b, w_conv_o, b_conv_o, w_out, ln1_g, ln1_b, peer_wq, peer_keys, peer_u, peer_v, ln2_g, ln2_b):
    bsz, seq, d = x.shape
    n_meta = meta.shape[0]
    depth = w_in.shape[0]
    kw = w_a2.shape[2]
    vw = w_gla_o.shape[1]
    rank = w_a2.shape[1]
    cc = conv_w.shape[2]
    assert seq % CHUNK == 0 and 0 < n_meta <= CHUNK
    assert depth == 1, "zero pad rows are only maintained for a single layer"
    lp = -(-(CHUNK + seq) // ROW_ALIGN) * ROW_ALIGN
    alpha = (2.0 * depth) ** 0.25
    off_a = 2 * kw + 2 * vw
    off_c = off_a + rank
    off_gate = off_c + 2 * cc

    hf, hb = _ln0(x, meta, ln0_g, ln0_b, lp)
    for l in range(depth):
        wl = w_in[l]
        w_qkvg = wl[:, :off_a].astype(BF16)
        w_a = jnp.pad(wl[:, off_a:off_c], ((0, 0), (0, 128 - rank))).astype(BF16)
        w_c = wl[:, off_c:off_gate].astype(BF16)
        w_gate = wl[:, off_gate:].astype(BF16)

        qkvg = _matmul(hb, w_qkvg, BF16, name="proj_qkvg")
        a_lr = _matmul(hb, w_a, F32, name="proj_gate_lowrank")
        o_gated = _gla(qkvg, a_lr, w_a2[l], b_a[l], gla_norm_g[l], bsz, lp)

        c = _glu_matmul(hb, w_c, cc)
        c_act = _conv(c, conv_w[l], conv_b[l], conv_ln_g[l], conv_ln_b[l])

        gates = _matmul(hb, w_gate, BF16, act="sigmoid", name="proj_branch_gates")
        merged = _merge(o_gated, c_act, w_gla_o[l].astype(BF16), w_conv_o[l].astype(BF16), b_conv_o[l], gates)
        mix = _matmul(merged, w_out[l].astype(BF16), F32, name="proj_out")
        h1, h1t = _res_ln(hf, mix, ln1_g[l], ln1_b[l], alpha, transposed_copy=True)

        heads, _, nk, _ = peer_keys[l].shape
        qt = _matmul(peer_wq[l].T.astype(BF16), h1t, BF16, name="peer_query")
        sc, st = _peer_scores(qt, peer_keys[l].astype(BF16))
        ffn = _peer_dense(h1t, peer_u[l].astype(BF16), peer_v[l].astype(BF16), sc, st, heads, nk)
    return _final_res_ln(h1, ffn, ln2_g[0], ln2_b[0], alpha, bsz, lp, seq)
```

```python
import functools
import math

import jax
import jax.numpy as jnp
from jax import lax
from jax.experimental import pallas as pl
from jax.experimental.pallas import tpu as pltpu

F32 = jnp.float32
BF16 = jnp.bfloat16

CHUNK = 64
GLA_DK = 128
GLA_DV = 256
GATE_TAU = 16.0
PEER_TOPK = 16
LN_EPS = 1e-5
RMS_EPS = 1e-6
SUBLANES = 8
LANES = 128
PEER_STRIP = 64
CONV_ROWS = 128
ROW_ALIGN = 256
V7X_VMEM_BYTES = 64 * 1024 * 1024
VMEM_CAP = V7X_VMEM_BYTES - 8 * 1024 * 1024


def _pick(n, prefs):
    for p in prefs:
        if n % p == 0:
            return p
    return n


def _params(sem, vmem_bytes):
    return pltpu.CompilerParams(dimension_semantics=sem,
                                vmem_limit_bytes=int(min(max(vmem_bytes, 16 * 1024 * 1024), VMEM_CAP)))


def _ln_rows(x, g, b):
    mu = jnp.mean(x, axis=-1, keepdims=True)
    xc = x - mu
    var = jnp.mean(xc * xc, axis=-1, keepdims=True)
    return xc * lax.rsqrt(var + LN_EPS) * g + b


def _sigmoid(x):
    return 1.0 / (1.0 + jnp.exp(-x))


def _ln0_kernel(x_ref, meta_ref, g_ref, b_ref, h_ref, *, n_meta, n_xt):
    i = pl.program_id(1)
    g = g_ref[...]
    b = b_ref[...]
    front = h_ref.shape[1]

    @pl.when(i == 0)
    def _():
        pad = front - n_meta
        y = _ln_rows(meta_ref[...], g, b)
        h_ref[0, 0:pad, :] = jnp.zeros((pad, y.shape[1]), BF16)
        h_ref[0, pad:front, :] = y.astype(BF16)

    @pl.when(jnp.logical_and(i >= 1, i <= n_xt))
    def _():
        h_ref[0] = _ln_rows(x_ref[0], g, b).astype(BF16)

    @pl.when(i > n_xt)
    def _():
        h_ref[...] = jnp.zeros_like(h_ref)


def _ln0(x, meta, g, b, front, lp):
    bsz, seq, d = x.shape
    n_meta = meta.shape[0]
    n_xt = seq // front
    h = pl.pallas_call(
        functools.partial(_ln0_kernel, n_meta=n_meta, n_xt=n_xt),
        grid=(bsz, lp // front),
        in_specs=[
            pl.BlockSpec((1, front, d), lambda bb, i: (bb, jnp.clip(i - 1, 0, n_xt - 1), 0)),
            pl.BlockSpec((n_meta, d), lambda bb, i: (0, 0)),
            pl.BlockSpec((1, d), lambda bb, i: (0, 0)),
            pl.BlockSpec((1, d), lambda bb, i: (0, 0)),
        ],
        out_specs=pl.BlockSpec((1, front, d), lambda bb, i: (bb, i, 0)),
        out_shape=jax.ShapeDtypeStruct((bsz, lp, d), BF16),
        compiler_params=_params(("parallel", "arbitrary"), 24 * front * d),
        name="ln0",
    )(x, meta, g.reshape(1, d), b.reshape(1, d))
    return h.reshape(bsz * lp, d)


def _mm_kernel(a_ref, b_ref, o_ref, *, act):
    acc = jnp.dot(a_ref[...], b_ref[...], preferred_element_type=F32)
    if act == "sigmoid":
        acc = _sigmoid(acc)
    o_ref[...] = acc.astype(o_ref.dtype)


def _matmul(a, b, out_dtype, *, act=None, name, col0=0, n=None):
    m, k = a.shape
    n = b.shape[1] if n is None else n
    tm = _pick(m, (768, 512, 256, 128))
    tn = _pick(math.gcd(n, col0) if col0 else n, (1024, 512, 256, 128))
    assert n % tn == 0 and col0 % tn == 0
    cb = col0 // tn
    osz = jnp.dtype(out_dtype).itemsize
    vmem = 2 * (tm * k * 2 + k * tn * 2 + tm * tn * osz) + 5 * tm * tn * 4
    return pl.pallas_call(
        functools.partial(_mm_kernel, act=act),
        grid=(m // tm, n // tn),
        in_specs=[pl.BlockSpec((tm, k), lambda i, j: (i, 0)),
                  pl.BlockSpec((k, tn), lambda i, j: (0, j + cb))],
        out_specs=pl.BlockSpec((tm, tn), lambda i, j: (i, j)),
        out_shape=jax.ShapeDtypeStruct((m, n), out_dtype),
        compiler_params=_params(("parallel", "arbitrary"), vmem),
        name=name,
    )(a, b)


def _glu_kernel(a_ref, b1_ref, b2_ref, o_ref):
    a = a_ref[...]
    d1 = jnp.dot(a, b1_ref[...], preferred_element_type=F32)
    d2 = jnp.dot(a, b2_ref[...], preferred_element_type=F32)
    o_ref[...] = (d1 * _sigmoid(d2)).astype(o_ref.dtype)


def _glu_matmul(a, b, col0, n):
    m, k = a.shape
    tm = _pick(m, (768, 512, 256, 128))
    tn = _pick(math.gcd(n, col0) if col0 else n, (512, 256, 128))
    assert n % tn == 0 and col0 % tn == 0
    nb = n // tn
    cb = col0 // tn
    vmem = 2 * (tm * k * 2 + 2 * k * tn * 2 + tm * tn * 2) + 8 * tm * tn * 4
    return pl.pallas_call(
        _glu_kernel,
        grid=(m // tm, nb),
        in_specs=[pl.BlockSpec((tm, k), lambda i, j: (i, 0)),
                  pl.BlockSpec((k, tn), lambda i, j: (0, j + cb)),
                  pl.BlockSpec((k, tn), lambda i, j: (0, j + cb + nb))],
        out_specs=pl.BlockSpec((tm, tn), lambda i, j: (i, j)),
        out_shape=jax.ShapeDtypeStruct((m, n), BF16),
        compiler_params=_params(("parallel", "arbitrary"), vmem),
        name="glu_proj",
    )(a, b, b)


def _gla_kernel(q_ref, k_ref, v_ref, g_ref, a_ref, wa2_ref, ba_ref, gn_ref, o_ref, st_ref, *, heads, rank):
    c = pl.program_id(1)

    @pl.when(c == 0)
    def _():
        st_ref[...] = jnp.zeros_like(st_ref)

    a_lr = a_ref[:, 0:rank]
    z = jnp.dot(a_lr, wa2_ref[...], preferred_element_type=F32,
                precision=lax.Precision.HIGHEST) + ba_ref[...]
    log_a = jax.nn.log_sigmoid(z) / GATE_TAU
    row = lax.broadcasted_iota(jnp.int32, (CHUNK, CHUNK), 0)
    col = lax.broadcasted_iota(jnp.int32, (CHUNK, CHUNK), 1)
    tri = (row >= col).astype(F32)
    cum = jnp.dot(tri, log_a, preferred_element_type=F32, precision=lax.Precision.HIGHEST)
    tot = cum[CHUNK - 1:CHUNK, :]
    k_scale = jnp.exp(tot - cum)
    decay = jnp.exp(tot)
    gn = gn_ref[...]
    q_scale = GLA_DK ** -0.5
    for h in range(heads):
        sk = slice(h * GLA_DK, (h + 1) * GLA_DK)
        sv = slice(h * GLA_DV, (h + 1) * GLA_DV)
        kd = (k_ref[:, sk].astype(F32) * k_scale[:, sk]).astype(BF16)
        upd = lax.dot_general(v_ref[:, sv], kd, (((0,), (0,)), ((), ())), preferred_element_type=F32)
        st = decay[:, sk] * st_ref[h] + upd
        st_ref[h] = st
        q = (q_ref[:, sk].astype(F32) * q_scale).astype(BF16)
        o = lax.dot_general(q, st.astype(BF16), (((1,), (1,)), ((), ())), preferred_element_type=F32)
        y = o * lax.rsqrt(jnp.mean(o * o, axis=-1, keepdims=True) + RMS_EPS) * gn
        gg = g_ref[:, sv].astype(F32)
        o_ref[:, sv] = (y * (gg * _sigmoid(gg))).astype(o_ref.dtype)


def _gla(qkvg, a_lr, w_a2, b_a, gn, bsz, lp):
    t, _ = qkvg.shape
    kw = w_a2.shape[1]
    heads = kw // GLA_DK
    vw = heads * GLA_DV
    rank = w_a2.shape[0]
    nc = lp // CHUNK
    aw = a_lr.shape[1]
    assert kw == vw // 2
    row = lambda bb, c: bb * nc + c
    vmem = 2 * (2 * CHUNK * kw * 2 + 3 * CHUNK * vw * 2) + heads * GLA_DK * GLA_DV * 4 + 12 * CHUNK * kw * 4
    return pl.pallas_call(
        functools.partial(_gla_kernel, heads=heads, rank=rank),
        grid=(bsz, nc),
        in_specs=[
            pl.BlockSpec((CHUNK, kw), lambda bb, c: (row(bb, c), 0)),
            pl.BlockSpec((CHUNK, kw), lambda bb, c: (row(bb, c), 1)),
            pl.BlockSpec((CHUNK, vw), lambda bb, c: (row(bb, c), 1)),
            pl.BlockSpec((CHUNK, vw), lambda bb, c: (row(bb, c), 2)),
            pl.BlockSpec((CHUNK, aw), lambda bb, c: (row(bb, c), 0)),
            pl.BlockSpec((rank, kw), lambda bb, c: (0, 0)),
            pl.BlockSpec((1, kw), lambda bb, c: (0, 0)),
            pl.BlockSpec((1, GLA_DV), lambda bb, c: (0, 0)),
        ],
        out_specs=pl.BlockSpec((CHUNK, vw), lambda bb, c: (row(bb, c), 0)),
        out_shape=jax.ShapeDtypeStruct((t, vw), BF16),
        scratch_shapes=[pltpu.VMEM((heads, GLA_DV, GLA_DK), F32)],
        compiler_params=_params(("parallel", "arbitrary"), vmem),
        name="gla",
    )(qkvg, qkvg, qkvg, qkvg, a_lr, w_a2, b_a.reshape(1, kw), gn.reshape(1, GLA_DV))


def _conv_kernel(c_ref, halo_ref, w_ref, b_ref, lg_ref, lb_ref, o_ref, buf_ref, y_ref, *, tc, hb, width):
    i = pl.program_id(0)
    d = c_ref.shape[1]
    halo = halo_ref[...].astype(F32)
    buf_ref[0:hb, :] = jnp.where(i == 0, 0.0, halo)
    buf_ref[hb:hb + tc, :] = c_ref[...].astype(F32)
    buf_ref[hb + tc:hb + tc + SUBLANES, :] = jnp.zeros((SUBLANES, d), F32)
    base = hb - (width - 1)
    rows = CONV_ROWS
    n_rc = tc // rows

    def chunk(t, carry):
        r0 = pl.multiple_of((t % n_rc) * rows, rows)
        l0 = pl.multiple_of((t // n_rc) * LANES, LANES)
        acc = jnp.zeros((rows, LANES), F32) + b_ref[:, pl.ds(l0, LANES)]
        for s in range(SUBLANES):
            part = None
            for j in range(width):
                if (base + j) % SUBLANES != s:
                    continue
                a0 = ((base + j) // SUBLANES) * SUBLANES
                term = w_ref[j:j + 1, pl.ds(l0, LANES)] * buf_ref[pl.ds(r0 + a0, rows + SUBLANES), pl.ds(l0, LANES)]
                part = term if part is None else part + term
            if part is not None:
                acc = acc + part[s:s + rows, :]
        y_ref[pl.ds(r0, rows), pl.ds(l0, LANES)] = acc
        return carry

    lax.fori_loop(0, n_rc * (d // LANES), chunk, 0)
    y = _ln_rows(y_ref[...], lg_ref[...], lb_ref[...])
    o_ref[...] = (y * _sigmoid(y)).astype(o_ref.dtype)


def _conv(c, w, b, lg, lb):
    t, d = c.shape
    width = w.shape[0]
    hb = 32
    tc = _pick(t, (256, 128))
    assert width - 1 <= hb <= CHUNK and tc % CONV_ROWS == 0 and d % LANES == 0
    r = tc // hb
    vmem = 2 * (tc * d * 2 * 2 + hb * d * 2) + (2 * tc + hb + SUBLANES) * d * 4 + 6 * tc * d * 4
    return pl.pallas_call(
        functools.partial(_conv_kernel, tc=tc, hb=hb, width=width),
        grid=(t // tc,),
        in_specs=[
            pl.BlockSpec((tc, d), lambda i: (i, 0)),
            pl.BlockSpec((hb, d), lambda i: (jnp.maximum(i * r - 1, 0), 0)),
            pl.BlockSpec((width, d), lambda i: (0, 0)),
            pl.BlockSpec((1, d), lambda i: (0, 0)),
            pl.BlockSpec((1, d), lambda i: (0, 0)),
            pl.BlockSpec((1, d), lambda i: (0, 0)),
        ],
        out_specs=pl.BlockSpec((tc, d), lambda i: (i, 0)),
        out_shape=jax.ShapeDtypeStruct((t, d), BF16),
        scratch_shapes=[pltpu.VMEM((tc + hb + SUBLANES, d), F32), pltpu.VMEM((tc, d), F32)],
        compiler_params=_params(("parallel",), vmem),
        name="conv_ln_silu",
    )(c, c, w, b.reshape(1, d), lg.reshape(1, d), lb.reshape(1, d))


def _merge_kernel(o_ref_in, c_ref, wg_ref, wc_ref, bc_ref, gg_ref, gc_ref, out_ref):
    yg = jnp.dot(o_ref_in[...], wg_ref[...], preferred_element_type=F32)
    yc = jnp.dot(c_ref[...], wc_ref[...], preferred_element_type=F32) + bc_ref[...]
    out = gg_ref[...].astype(F32) * yg + gc_ref[...].astype(F32) * yc
    out_ref[...] = out.astype(out_ref.dtype)


def _merge(o_gated, c_act, w_gla_o, w_conv_o, b_conv_o, gates):
    m, k1 = o_gated.shape
    k2 = c_act.shape[1]
    n = w_gla_o.shape[1]
    tm = _pick(m, (768, 512, 256, 128))
    tn = _pick(n, (512, 256, 128))
    nb = n // tn
    vmem = 2 * (tm * (k1 + k2) * 2 + (k1 + k2) * tn * 2 + 3 * tm * tn * 2) + 4 * tm * tn * 4
    return pl.pallas_call(
        _merge_kernel,
        grid=(m // tm, nb),
        in_specs=[
            pl.BlockSpec((tm, k1), lambda i, j: (i, 0)),
            pl.BlockSpec((tm, k2), lambda i, j: (i, 0)),
            pl.BlockSpec((k1, tn), lambda i, j: (0, j)),
            pl.BlockSpec((k2, tn), lambda i, j: (0, j)),
            pl.BlockSpec((1, tn), lambda i, j: (0, j)),
            pl.BlockSpec((tm, tn), lambda i, j: (i, j)),
            pl.BlockSpec((tm, tn), lambda i, j: (i, j + nb)),
        ],
        out_specs=pl.BlockSpec((tm, tn), lambda i, j: (i, j)),
        out_shape=jax.ShapeDtypeStruct((m, n), BF16),
        compiler_params=_params(("parallel", "arbitrary"), vmem),
        name="branch_merge",
    )(o_gated, c_act, w_gla_o, w_conv_o, b_conv_o.reshape(1, n), gates, gates)


def _resln_kernel(h_ref, y_ref, g_ref, b_ref, of_ref, ot_ref, *, alpha):
    y = _ln_rows(alpha * h_ref[...].astype(F32) + y_ref[...], g_ref[...], b_ref[...])
    of_ref[...] = y
    ot_ref[...] = y.T.astype(BF16)


def _res_ln(h, y, g, b, alpha):
    t, d = h.shape
    tr = _pick(t, (256, 128))
    out_specs = [pl.BlockSpec((tr, d), lambda i: (i, 0)), pl.BlockSpec((d, tr), lambda i: (0, i))]
    out_shape = [jax.ShapeDtypeStruct((t, d), F32), jax.ShapeDtypeStruct((d, t), BF16)]
    return pl.pallas_call(
        functools.partial(_resln_kernel, alpha=alpha),
        grid=(t // tr,),
        in_specs=[pl.BlockSpec((tr, d), lambda i: (i, 0)),
                  pl.BlockSpec((tr, d), lambda i: (i, 0)),
                  pl.BlockSpec((1, d), lambda i: (0, 0)),
                  pl.BlockSpec((1, d), lambda i: (0, 0))],
        out_specs=out_specs,
        out_shape=out_shape,
        compiler_params=_params(("parallel",), 16 * tr * d * 4),
        name="res_ln_t",
    )(h, y, g.reshape(1, d), b.reshape(1, d))


def _final_ln_kernel(h_ref, y_ref, g_ref, b_ref, o_ref, *, alpha):
    o_ref[0] = _ln_rows(alpha * h_ref[...] + y_ref[...], g_ref[...], b_ref[...])


def _final_res_ln(h, y, g, b, alpha, bsz, front, lp, seq):
    d = h.shape[1]
    nt = lp // front
    row = lambda bb, c: (bb * nt + c + 1, 0)
    return pl.pallas_call(
        functools.partial(_final_ln_kernel, alpha=alpha),
        grid=(bsz, seq // front),
        in_specs=[pl.BlockSpec((front, d), row),
                  pl.BlockSpec((front, d), row),
                  pl.BlockSpec((1, d), lambda bb, c: (0, 0)),
                  pl.BlockSpec((1, d), lambda bb, c: (0, 0))],
        out_specs=pl.BlockSpec((1, front, d), lambda bb, c: (bb, c, 0)),
        out_shape=jax.ShapeDtypeStruct((bsz, seq, d), F32),
        compiler_params=_params(("parallel", "parallel"), 16 * front * d * 4),
        name="final_res_ln",
    )(h, y, g.reshape(1, d), b.reshape(1, d))


def _sort_desc(v):
    n = len(v)
    k = 2
    while k <= n:
        j = k // 2
        while j >= 1:
            for i in range(n):
                l = i ^ j
                if l > i:
                    hi, lo = jnp.maximum(v[i], v[l]), jnp.minimum(v[i], v[l])
                    v[i], v[l] = (hi, lo) if (i & k) == 0 else (lo, hi)
            j //= 2
        k *= 2
    return v


def _merge_desc(v):
    n = len(v)
    j = n // 2
    while j >= 1:
        for i in range(n):
            l = i ^ j
            if l > i:
                v[i], v[l] = jnp.maximum(v[i], v[l]), jnp.minimum(v[i], v[l])
        j //= 2
    return v


def _merge_top(a, b):
    n = len(a)
    return _merge_desc([jnp.maximum(a[i], b[n - 1 - i]) for i in range(n)])


def _pad_pow2(v, n, fill):
    m = max(n, 1 << (len(v) - 1).bit_length())
    return v + [fill] * (m - len(v))


def _peer_score_kernel(qt_ref, keys_ref, sc_ref, st_ref, *, heads, nk, half, topk):
    tq = qt_ref.shape[1]
    for idx in range(2 * heads):
        sc_ref[idx * nk:(idx + 1) * nk, :] = jnp.dot(
            keys_ref[idx], qt_ref[idx * half:(idx + 1) * half, :], preferred_element_type=F32)
    sub = lax.broadcasted_iota(jnp.int32, (SUBLANES, LANES), 0)
    ninf = jnp.full((SUBLANES, LANES), -jnp.inf, F32)
    for lt in range(tq // LANES):
        ls = slice(lt * LANES, (lt + 1) * LANES)
        packed = []
        for p in range(2):
            pk = [ninf] * topk
            for h in range(heads):
                r0 = (2 * h + p) * nk
                v = [sc_ref[r0 + SUBLANES * g:r0 + SUBLANES * (g + 1), ls] for g in range(nk // SUBLANES)]
                v = _sort_desc(_pad_pow2(v, topk, ninf))[:topk]
                shift = SUBLANES // 2
                while shift >= 1:
                    v = _merge_top(v, [pltpu.roll(x, shift, 0) for x in v])
                    shift //= 2
                pk = [jnp.where(sub == h, v[a], pk[a]) for a in range(topk)]
            packed.append(pk)
        t1, t2 = packed
        lists = [[t1[a] + t2[b] for b in range(topk // (a + 1))] for a in range(topk)]
        best = lists[0]
        rest = [x for l in lists[2:] for x in l]
        if len(lists) > 1:
            best = _merge_top(best, lists[1] + [ninf] * (topk - len(lists[1])))
        if rest:
            best = _merge_top(best, _sort_desc(_pad_pow2(rest, topk, ninf))[:topk])
        m = t1[0] + t2[0]
        z = jnp.exp(best[0] - m)
        for a in range(1, topk):
            z = z + jnp.exp(best[a] - m)
        st_ref[0 * SUBLANES:1 * SUBLANES, ls] = best[topk - 1]
        st_ref[1 * SUBLANES:2 * SUBLANES, ls] = t1[0]
        st_ref[2 * SUBLANES:3 * SUBLANES, ls] = t2[0]
        st_ref[3 * SUBLANES:4 * SUBLANES, ls] = 1.0 / z


def _peer_scores(qt, keys):
    heads, _, nk, half = keys.shape
    t = qt.shape[1]
    tq = _pick(t, (256, 128))
    keys_f = keys.reshape(heads * 2, nk, half)
    assert heads <= SUBLANES and nk % SUBLANES == 0 and (PEER_TOPK & (PEER_TOPK - 1)) == 0
    return pl.pallas_call(
        functools.partial(_peer_score_kernel, heads=heads, nk=nk, half=half, topk=PEER_TOPK),
        grid=(t // tq,),
        in_specs=[pl.BlockSpec((heads * 2 * half, tq), lambda i: (0, i)),
                  pl.BlockSpec((heads * 2, nk, half), lambda i: (0, 0, 0))],
        out_specs=[pl.BlockSpec((heads * 2 * nk, tq), lambda i: (0, i)),
                   pl.BlockSpec((4 * SUBLANES, tq), lambda i: (0, i))],
        out_shape=[jax.ShapeDtypeStruct((heads * 2 * nk, t), F32),
                   jax.ShapeDtypeStruct((4 * SUBLANES, t), F32)],
        compiler_params=_params(("parallel",), 32 * 1024 * 1024),
        name="peer_scores",
    )(qt, keys_f)


def _peer_dense_kernel(xt_ref, u_ref, v_ref, sc_ref, st_ref, o_ref, e_ref, act_ref, p_ref, *, heads, nk, te):
    j = pl.program_id(1)
    tq = xt_ref.shape[1]
    nch = te // nk

    @pl.when(j == 0)
    def _():
        o_ref[...] = jnp.zeros_like(o_ref)
        for h in range(heads):
            m1 = st_ref[SUBLANES + h:SUBLANES + h + 1, :]
            m2 = st_ref[2 * SUBLANES + h:2 * SUBLANES + h + 1, :]
            rz = st_ref[3 * SUBLANES + h:3 * SUBLANES + h + 1, :]
            r1 = slice((2 * h) * nk, (2 * h + 1) * nk)
            r2 = slice((2 * h + 1) * nk, (2 * h + 2) * nk)
            e_ref[r1, :] = jnp.exp(sc_ref[r1, :] - m1)
            e_ref[r2, :] = jnp.exp(sc_ref[r2, :] - m2) * rz

    act_ref[...] = jnp.dot(u_ref[...], xt_ref[...], preferred_element_type=F32)

    s1_rows = [[sc_ref[pl.ds((2 * h) * nk + j * nch + il, 1), :] for h in range(heads)] for il in range(nch)]
    e1_rows = [[e_ref[pl.ds((2 * h) * nk + j * nch + il, 1), :] for h in range(heads)] for il in range(nch)]
    for cs in range(tq // LANES):
        lanes = slice(cs * LANES, (cs + 1) * LANES)
        tau = [st_ref[h:h + 1, lanes] for h in range(heads)]
        s1 = [[r[:, lanes] for r in rows] for rows in s1_rows]
        e1 = [[r[:, lanes] for r in rows] for rows in e1_rows]
        for rs in range(nk // PEER_STRIP):
            accs = [None] * nch
            for h in range(heads):
                r2 = slice((2 * h + 1) * nk + rs * PEER_STRIP, (2 * h + 1) * nk + (rs + 1) * PEER_STRIP)
                s2 = sc_ref[r2, lanes]
                e2 = e_ref[r2, lanes]
                for il in range(nch):
                    wh = e1[il][h] * jnp.where((s1[il][h] + s2) >= tau[h], e2, 0.0)
                    accs[il] = wh if accs[il] is None else accs[il] + wh
            for il in range(nch):
                rows = slice(il * nk + rs * PEER_STRIP, il * nk + (rs + 1) * PEER_STRIP)
                a = act_ref[rows, lanes]
                g = 0.5 * a * (1.0 + lax.erf(a * (2.0 ** -0.5)))
                p_ref[rows, lanes] = (accs[il] * g).astype(BF16)

    o_ref[...] += lax.dot_general(p_ref[...], v_ref[...], (((0,), (0,)), ((), ())), preferred_element_type=F32)


def _peer_dense(xt, u, v, sc, st, heads, nk):
    d, t = xt.shape
    ne = u.shape[0]
    tq = _pick(t, (512, 256, 128))
    te = _pick(ne, (512, 256, 128))
    assert te % nk == 0 and ne == nk * nk and nk % PEER_STRIP == 0
    vmem = (d * tq * 2 + 2 * heads * nk * tq * 4 + 2 * 4 * SUBLANES * tq * 4 + 2 * (2 * te * d * 2 + tq * d * 4)
            + 2 * heads * nk * tq * 4 + te * tq * 6 + 4 * te * tq * 4)
    once = pl.Buffered(1)
    return pl.pallas_call(
        functools.partial(_peer_dense_kernel, heads=heads, nk=nk, te=te),
        grid=(t // tq, ne // te),
        in_specs=[pl.BlockSpec((d, tq), lambda i, j: (0, i), pipeline_mode=once),
                  pl.BlockSpec((te, d), lambda i, j: (j, 0)),
                  pl.BlockSpec((te, d), lambda i, j: (j, 0)),
                  pl.BlockSpec((2 * heads * nk, tq), lambda i, j: (0, i), pipeline_mode=once),
                  pl.BlockSpec((4 * SUBLANES, tq), lambda i, j: (0, i))],
        out_specs=pl.BlockSpec((tq, d), lambda i, j: (i, 0)),
        out_shape=jax.ShapeDtypeStruct((t, d), F32),
        scratch_shapes=[pltpu.VMEM((2 * heads * nk, tq), F32),
                        pltpu.VMEM((te, tq), F32),
                        pltpu.VMEM((te, tq), BF16)],
        compiler_params=_params(("parallel", "arbitrary"), vmem),
        name="peer_dense",
    )(xt, u, v, sc, st)


def kernel(x, meta, ln0_g, ln0_b, w_in, w_a2, b_a, gla_norm_g, w_gla_o, conv_w, conv_b, conv_ln_g, conv_ln_b, w_conv_o, b_conv_o, w_out, ln1_g, ln1_b, peer_wq, peer_keys, peer_u, peer_v, ln2_g, ln2_b):
    bsz, seq, d = x.shape
    n_meta = meta.shape[0]
    depth = w_in.shape[0]
    kw = w_a2.shape[2]
    vw = w_gla_o.shape[1]
    rank = w_a2.shape[1]
    cc = conv_w.shape[2]
    front = _pick(seq, (ROW_ALIGN, 128, CHUNK))
    assert seq % front == 0 and front % CHUNK == 0 and 0 < n_meta <= CHUNK
    assert depth == 1, "zero pad rows are only maintained for a single layer"
    lp = -(-(front + seq) // ROW_ALIGN) * ROW_ALIGN
    alpha = (2.0 * depth) ** 0.25
    off_a = 2 * kw + 2 * vw
    off_c = off_a + rank
    wl = w_in[0]
    w_all = jnp.concatenate([wl[:, :off_a], wl[:, off_c:], jnp.pad(wl[:, off_a:off_c], ((0, 0), (0, LANES - rank)))],
                            axis=1).astype(BF16)
    col_c = off_a
    col_gate = col_c + 2 * cc
    col_a = col_gate + 2 * d

    hb = _ln0(x, meta, ln0_g, ln0_b, front, lp)
    qkvg = _matmul(hb, w_all, BF16, name="proj_qkvg", col0=0, n=off_a)
    a_lr = _matmul(hb, w_all, F32, name="proj_gate_lowrank", col0=col_a, n=LANES)
    o_gated = _gla(qkvg, a_lr, w_a2[0], b_a[0], gla_norm_g[0], bsz, lp)

    c = _glu_matmul(hb, w_all, col_c, cc)
    c_act = _conv(c, conv_w[0], conv_b[0], conv_ln_g[0], conv_ln_b[0])

    gates = _matmul(hb, w_all, BF16, act="sigmoid", name="proj_branch_gates", col0=col_gate, n=2 * d)
    merged = _merge(o_gated, c_act, w_gla_o[0].astype(BF16), w_conv_o[0].astype(BF16), b_conv_o[0], gates)
    mix = _matmul(merged, w_out[0].astype(BF16), F32, name="proj_out")
    h1, h1t = _res_ln(hb, mix, ln1_g[0], ln1_b[0], alpha)

    heads, _, nk, _ = peer_keys[0].shape
    qt = _matmul(peer_wq[0].T.astype(BF16), h1t, BF16, name="peer_query")
    sc, st = _peer_scores(qt, peer_keys[0].astype(BF16))
    ffn = _peer_dense(h1t, peer_u[0].astype(BF16), peer_v[0].astype(BF16), sc, st, heads, nk)
    return _final_res_ln(h1, ffn, ln2_g[0], ln2_b[0], alpha, bsz, front, lp, seq)
```

```python
import functools
import math

import jax
import jax.numpy as jnp
from jax import lax
from jax.experimental import pallas as pl
from jax.experimental.pallas import tpu as pltpu

F32 = jnp.float32
BF16 = jnp.bfloat16

CHUNK = 64
GLA_DK = 128
GLA_DV = 256
GATE_TAU = 16.0
PEER_TOPK = 16
LN_EPS = 1e-5
RMS_EPS = 1e-6
SUBLANES = 8
LANES = 128
PEER_STRIP = 64
CONV_ROWS = 128
ROW_ALIGN = 256
V7X_VMEM_BYTES = 64 * 1024 * 1024
VMEM_CAP = V7X_VMEM_BYTES - 8 * 1024 * 1024


def _pick(n, prefs):
    for p in prefs:
        if n % p == 0:
            return p
    return n


def _params(sem, vmem_bytes):
    return pltpu.CompilerParams(dimension_semantics=sem,
                                vmem_limit_bytes=int(min(max(vmem_bytes, 16 * 1024 * 1024), VMEM_CAP)))


def _ln_rows(x, g, b):
    mu = jnp.mean(x, axis=-1, keepdims=True)
    xc = x - mu
    var = jnp.mean(xc * xc, axis=-1, keepdims=True)
    return xc * lax.rsqrt(var + LN_EPS) * g + b


def _sigmoid(x):
    return 1.0 / (1.0 + jnp.exp(-x))


def _ln0_kernel(x_ref, meta_ref, g_ref, b_ref, h_ref, *, n_meta, n_xt):
    i = pl.program_id(1)
    g = g_ref[...]
    b = b_ref[...]
    front = h_ref.shape[1]

    @pl.when(i == 0)
    def _():
        pad = front - n_meta
        y = _ln_rows(meta_ref[...], g, b)
        h_ref[0, 0:pad, :] = jnp.zeros((pad, y.shape[1]), BF16)
        h_ref[0, pad:front, :] = y.astype(BF16)

    @pl.when(jnp.logical_and(i >= 1, i <= n_xt))
    def _():
        h_ref[0] = _ln_rows(x_ref[0], g, b).astype(BF16)

    @pl.when(i > n_xt)
    def _():
        h_ref[...] = jnp.zeros_like(h_ref)


def _ln0(x, meta, g, b, front, lp):
    bsz, seq, d = x.shape
    n_meta = meta.shape[0]
    n_xt = seq // front
    h = pl.pallas_call(
        functools.partial(_ln0_kernel, n_meta=n_meta, n_xt=n_xt),
        grid=(bsz, lp // front),
        in_specs=[
            pl.BlockSpec((1, front, d), lambda bb, i: (bb, jnp.clip(i - 1, 0, n_xt - 1), 0)),
            pl.BlockSpec((n_meta, d), lambda bb, i: (0, 0)),
            pl.BlockSpec((1, d), lambda bb, i: (0, 0)),
            pl.BlockSpec((1, d), lambda bb, i: (0, 0)),
        ],
        out_specs=pl.BlockSpec((1, front, d), lambda bb, i: (bb, i, 0)),
        out_shape=jax.ShapeDtypeStruct((bsz, lp, d), BF16),
        compiler_params=_params(("parallel", "arbitrary"), 24 * front * d),
        name="ln0",
    )(x, meta, g.reshape(1, d), b.reshape(1, d))
    return h.reshape(bsz * lp, d)


def _mm_kernel(a_ref, b_ref, o_ref, *, act):
    acc = jnp.dot(a_ref[...], b_ref[...], preferred_element_type=F32)
    if act == "sigmoid":
        acc = _sigmoid(acc)
    o_ref[...] = acc.astype(o_ref.dtype)


def _matmul(a, b, out_dtype, *, act=None, name, col0=0, n=None):
    m, k = a.shape
    n = b.shape[1] if n is None else n
    tm = _pick(m, (768, 512, 256, 128))
    tn = _pick(math.gcd(n, col0) if col0 else n, (1024, 512, 256, 128))
    assert n % tn == 0 and col0 % tn == 0
    cb = col0 // tn
    osz = jnp.dtype(out_dtype).itemsize
    vmem = 2 * (tm * k * 2 + k * tn * 2 + tm * tn * osz) + 5 * tm * tn * 4
    return pl.pallas_call(
        functools.partial(_mm_kernel, act=act),
        grid=(m // tm, n // tn),
        in_specs=[pl.BlockSpec((tm, k), lambda i, j: (i, 0)),
                  pl.BlockSpec((k, tn), lambda i, j: (0, j + cb))],
        out_specs=pl.BlockSpec((tm, tn), lambda i, j: (i, j)),
        out_shape=jax.ShapeDtypeStruct((m, n), out_dtype),
        compiler_params=_params(("parallel", "arbitrary"), vmem),
        name=name,
    )(a, b)


def _glu_kernel(a_ref, b1_ref, b2_ref, o_ref):
    a = a_ref[...]
    d1 = jnp.dot(a, b1_ref[...], preferred_element_type=F32)
    d2 = jnp.dot(a, b2_ref[...], preferred_element_type=F32)
    o_ref[...] = (d1 * _sigmoid(d2)).astype(o_ref.dtype)


def _glu_matmul(a, b, col0, n):
    m, k = a.shape
    tm = _pick(m, (768, 512, 256, 128))
    tn = _pick(math.gcd(n, col0) if col0 else n, (512, 256, 128))
    assert n % tn == 0 and col0 % tn == 0
    nb = n // tn
    cb = col0 // tn
    vmem = 2 * (tm * k * 2 + 2 * k * tn * 2 + tm * tn * 2) + 8 * tm * tn * 4
    return pl.pallas_call(
        _glu_kernel,
        grid=(m // tm, nb),
        in_specs=[pl.BlockSpec((tm, k), lambda i, j: (i, 0)),
                  pl.BlockSpec((k, tn), lambda i, j: (0, j + cb)),
                  pl.BlockSpec((k, tn), lambda i, j: (0, j + cb + nb))],
        out_specs=pl.BlockSpec((tm, tn), lambda i, j: (i, j)),
        out_shape=jax.ShapeDtypeStruct((m, n), BF16),
        compiler_params=_params(("parallel", "arbitrary"), vmem),
        name="glu_proj",
    )(a, b, b)


def _gla_kernel(q_ref, k_ref, v_ref, g_ref, a_ref, wa2_ref, ba_ref, gn_ref, o_ref, st_ref, *, heads, rank):
    c = pl.program_id(1)

    @pl.when(c == 0)
    def _():
        st_ref[...] = jnp.zeros_like(st_ref)

    a_lr = a_ref[:, 0:rank]
    z = jnp.dot(a_lr, wa2_ref[...], preferred_element_type=F32,
                precision=lax.Precision.HIGHEST) + ba_ref[...]
    log_a = jax.nn.log_sigmoid(z) / GATE_TAU
    row = lax.broadcasted_iota(jnp.int32, (CHUNK, CHUNK), 0)
    col = lax.broadcasted_iota(jnp.int32, (CHUNK, CHUNK), 1)
    tri = (row >= col).astype(F32)
    cum = jnp.dot(tri, log_a, preferred_element_type=F32, precision=lax.Precision.HIGHEST)
    tot = cum[CHUNK - 1:CHUNK, :]
    k_scale = jnp.exp(tot - cum)
    decay = jnp.exp(tot)
    gn = gn_ref[...]
    q_scale = GLA_DK ** -0.5
    for h in range(heads):
        sk = slice(h * GLA_DK, (h + 1) * GLA_DK)
        sv = slice(h * GLA_DV, (h + 1) * GLA_DV)
        kd = (k_ref[:, sk].astype(F32) * k_scale[:, sk]).astype(BF16)
        upd = lax.dot_general(v_ref[:, sv], kd, (((0,), (0,)), ((), ())), preferred_element_type=F32)
        st = decay[:, sk] * st_ref[h] + upd
        st_ref[h] = st
        q = (q_ref[:, sk].astype(F32) * q_scale).astype(BF16)
        o = lax.dot_general(q, st.astype(BF16), (((1,), (1,)), ((), ())), preferred_element_type=F32)
        y = o * lax.rsqrt(jnp.mean(o * o, axis=-1, keepdims=True) + RMS_EPS) * gn
        gg = g_ref[:, sv].astype(F32)
        o_ref[:, sv] = (y * (gg * _sigmoid(gg))).astype(o_ref.dtype)


def _gla(qkvg, a_lr, w_a2, b_a, gn, bsz, lp):
    t, _ = qkvg.shape
    kw = w_a2.shape[1]
    heads = kw // GLA_DK
    vw = heads * GLA_DV
    rank = w_a2.shape[0]
    nc = lp // CHUNK
    aw = a_lr.shape[1]
    assert kw == vw // 2
    row = lambda bb, c: bb * nc + c
    vmem = 2 * (2 * CHUNK * kw * 2 + 3 * CHUNK * vw * 2) + heads * GLA_DK * GLA_DV * 4 + 12 * CHUNK * kw * 4
    return pl.pallas_call(
        functools.partial(_gla_kernel, heads=heads, rank=rank),
        grid=(bsz, nc),
        in_specs=[
            pl.BlockSpec((CHUNK, kw), lambda bb, c: (row(bb, c), 0)),
            pl.BlockSpec((CHUNK, kw), lambda bb, c: (row(bb, c), 1)),
            pl.BlockSpec((CHUNK, vw), lambda bb, c: (row(bb, c), 1)),
            pl.BlockSpec((CHUNK, vw), lambda bb, c: (row(bb, c), 2)),
            pl.BlockSpec((CHUNK, aw), lambda bb, c: (row(bb, c), 0)),
            pl.BlockSpec((rank, kw), lambda bb, c: (0, 0)),
            pl.BlockSpec((1, kw), lambda bb, c: (0, 0)),
            pl.BlockSpec((1, GLA_DV), lambda bb, c: (0, 0)),
        ],
        out_specs=pl.BlockSpec((CHUNK, vw), lambda bb, c: (row(bb, c), 0)),
        out_shape=jax.ShapeDtypeStruct((t, vw), BF16),
        scratch_shapes=[pltpu.VMEM((heads, GLA_DV, GLA_DK), F32)],
        compiler_params=_params(("parallel", "arbitrary"), vmem),
        name="gla",
    )(qkvg, qkvg, qkvg, qkvg, a_lr, w_a2, b_a.reshape(1, kw), gn.reshape(1, GLA_DV))


def _conv_kernel(c_ref, halo_ref, w_ref, b_ref, lg_ref, lb_ref, o_ref, buf_ref, y_ref, *, tc, hb, width):
    i = pl.program_id(0)
    d = c_ref.shape[1]
    halo = halo_ref[...].astype(F32)
    buf_ref[0:hb, :] = jnp.where(i == 0, 0.0, halo)
    buf_ref[hb:hb + tc, :] = c_ref[...].astype(F32)
    buf_ref[hb + tc:hb + tc + SUBLANES, :] = jnp.zeros((SUBLANES, d), F32)
    base = hb - (width - 1)
    rows = CONV_ROWS
    n_rc = tc // rows

    def chunk(t, carry):
        r0 = pl.multiple_of((t % n_rc) * rows, rows)
        l0 = pl.multiple_of((t // n_rc) * LANES, LANES)
        acc = jnp.zeros((rows, LANES), F32) + b_ref[:, pl.ds(l0, LANES)]
        for s in range(SUBLANES):
            part = None
            for j in range(width):
                if (base + j) % SUBLANES != s:
                    continue
                a0 = ((base + j) // SUBLANES) * SUBLANES
                term = w_ref[j:j + 1, pl.ds(l0, LANES)] * buf_ref[pl.ds(r0 + a0, rows + SUBLANES), pl.ds(l0, LANES)]
                part = term if part is None else part + term
            if part is not None:
                acc = acc + part[s:s + rows, :]
        y_ref[pl.ds(r0, rows), pl.ds(l0, LANES)] = acc
        return carry

    lax.fori_loop(0, n_rc * (d // LANES), chunk, 0)
    y = _ln_rows(y_ref[...], lg_ref[...], lb_ref[...])
    o_ref[...] = (y * _sigmoid(y)).astype(o_ref.dtype)


def _conv(c, w, b, lg, lb):
    t, d = c.shape
    width = w.shape[0]
    hb = 32
    tc = _pick(t, (256, 128))
    assert width - 1 <= hb <= CHUNK and tc % CONV_ROWS == 0 and d % LANES == 0
    r = tc // hb
    vmem = 2 * (tc * d * 2 * 2 + hb * d * 2) + (2 * tc + hb + SUBLANES) * d * 4 + 6 * tc * d * 4
    return pl.pallas_call(
        functools.partial(_conv_kernel, tc=tc, hb=hb, width=width),
        grid=(t // tc,),
        in_specs=[
            pl.BlockSpec((tc, d), lambda i: (i, 0)),
            pl.BlockSpec((hb, d), lambda i: (jnp.maximum(i * r - 1, 0), 0)),
            pl.BlockSpec((width, d), lambda i: (0, 0)),
            pl.BlockSpec((1, d), lambda i: (0, 0)),
            pl.BlockSpec((1, d), lambda i: (0, 0)),
            pl.BlockSpec((1, d), lambda i: (0, 0)),
        ],
        out_specs=pl.BlockSpec((tc, d), lambda i: (i, 0)),
        out_shape=jax.ShapeDtypeStruct((t, d), BF16),
        scratch_shapes=[pltpu.VMEM((tc + hb + SUBLANES, d), F32), pltpu.VMEM((tc, d), F32)],
        compiler_params=_params(("parallel",), vmem),
        name="conv_ln_silu",
    )(c, c, w, b.reshape(1, d), lg.reshape(1, d), lb.reshape(1, d))


def _merge_kernel(o_ref_in, c_ref, wg_ref, wc_ref, bc_ref, gg_ref, gc_ref, out_ref):
    yg = jnp.dot(o_ref_in[...], wg_ref[...], preferred_element_type=F32)
    yc = jnp.dot(c_ref[...], wc_ref[...], preferred_element_type=F32) + bc_ref[...]
    out = gg_ref[...].astype(F32) * yg + gc_ref[...].astype(F32) * yc
    out_ref[...] = out.astype(out_ref.dtype)


def _merge(o_gated, c_act, w_gla_o, w_conv_o, b_conv_o, gates):
    m, k1 = o_gated.shape
    k2 = c_act.shape[1]
    n = w_gla_o.shape[1]
    tm = _pick(m, (768, 512, 256, 128))
    tn = _pick(n, (512, 256, 128))
    nb = n // tn
    vmem = 2 * (tm * (k1 + k2) * 2 + (k1 + k2) * tn * 2 + 3 * tm * tn * 2) + 4 * tm * tn * 4
    return pl.pallas_call(
        _merge_kernel,
        grid=(m // tm, nb),
        in_specs=[
            pl.BlockSpec((tm, k1), lambda i, j: (i, 0)),
            pl.BlockSpec((tm, k2), lambda i, j: (i, 0)),
            pl.BlockSpec((k1, tn), lambda i, j: (0, j)),
            pl.BlockSpec((k2, tn), lambda i, j: (0, j)),
            pl.BlockSpec((1, tn), lambda i, j: (0, j)),
            pl.BlockSpec((tm, tn), lambda i, j: (i, j)),
            pl.BlockSpec((tm, tn), lambda i, j: (i, j + nb)),
        ],
        out_specs=pl.BlockSpec((tm, tn), lambda i, j: (i, j)),
        out_shape=jax.ShapeDtypeStruct((m, n), BF16),
        compiler_params=_params(("parallel", "arbitrary"), vmem),
        name="branch_merge",
    )(o_gated, c_act, w_gla_o, w_conv_o, b_conv_o.reshape(1, n), gates, gates)


def _resln_kernel(h_ref, y_ref, g_ref, b_ref, of_ref, ot_ref, *, alpha):
    y = _ln_rows(alpha * h_ref[...].astype(F32) + y_ref[...], g_ref[...], b_ref[...])
    of_ref[...] = y
    ot_ref[...] = y.T.astype(BF16)


def _res_ln(h, y, g, b, alpha):
    t, d = h.shape
    tr = _pick(t, (256, 128))
    out_specs = [pl.BlockSpec((tr, d), lambda i: (i, 0)), pl.BlockSpec((d, tr), lambda i: (0, i))]
    out_shape = [jax.ShapeDtypeStruct((t, d), F32), jax.ShapeDtypeStruct((d, t), BF16)]
    return pl.pallas_call(
        functools.partial(_resln_kernel, alpha=alpha),
        grid=(t // tr,),
        in_specs=[pl.BlockSpec((tr, d), lambda i: (i, 0)),
                  pl.BlockSpec((tr, d), lambda i: (i, 0)),
                  pl.BlockSpec((1, d), lambda i: (0, 0)),
                  pl.BlockSpec((1, d), lambda i: (0, 0))],
        out_specs=out_specs,
        out_shape=out_shape,
        compiler_params=_params(("parallel",), 16 * tr * d * 4),
        name="res_ln_t",
    )(h, y, g.reshape(1, d), b.reshape(1, d))


def _final_ln_kernel(h_ref, y_ref, g_ref, b_ref, o_ref, *, alpha):
    o_ref[0] = _ln_rows(alpha * h_ref[...] + y_ref[...], g_ref[...], b_ref[...])


def _final_res_ln(h, y, g, b, alpha, bsz, front, lp, seq):
    d = h.shape[1]
    nt = lp // front
    row = lambda bb, c: (bb * nt + c + 1, 0)
    return pl.pallas_call(
        functools.partial(_final_ln_kernel, alpha=alpha),
        grid=(bsz, seq // front),
        in_specs=[pl.BlockSpec((front, d), row),
                  pl.BlockSpec((front, d), row),
                  pl.BlockSpec((1, d), lambda bb, c: (0, 0)),
                  pl.BlockSpec((1, d), lambda bb, c: (0, 0))],
        out_specs=pl.BlockSpec((1, front, d), lambda bb, c: (bb, c, 0)),
        out_shape=jax.ShapeDtypeStruct((bsz, seq, d), F32),
        compiler_params=_params(("parallel", "parallel"), 16 * front * d * 4),
        name="final_res_ln",
    )(h, y, g.reshape(1, d), b.reshape(1, d))


def _sort_desc(v):
    n = len(v)
    k = 2
    while k <= n:
        j = k // 2
        while j >= 1:
            for i in range(n):
                l = i ^ j
                if l > i:
                    hi, lo = jnp.maximum(v[i], v[l]), jnp.minimum(v[i], v[l])
                    v[i], v[l] = (hi, lo) if (i & k) == 0 else (lo, hi)
            j //= 2
        k *= 2
    return v


def _merge_desc(v):
    n = len(v)
    j = n // 2
    while j >= 1:
        for i in range(n):
            l = i ^ j
            if l > i:
                v[i], v[l] = jnp.maximum(v[i], v[l]), jnp.minimum(v[i], v[l])
        j //= 2
    return v


def _merge_top(a, b):
    n = len(a)
    return _merge_desc([jnp.maximum(a[i], b[n - 1 - i]) for i in range(n)])


def _pad_pow2(v, n, fill):
    m = max(n, 1 << (len(v) - 1).bit_length())
    return v + [fill] * (m - len(v))


def _peer_score_kernel(qt_ref, keys_ref, sc_ref, st_ref, *, heads, nk, half, topk):
    tq = qt_ref.shape[1]
    for idx in range(2 * heads):
        sc_ref[idx * nk:(idx + 1) * nk, :] = jnp.dot(
            keys_ref[idx], qt_ref[idx * half:(idx + 1) * half, :], preferred_element_type=F32)
    sub = lax.broadcasted_iota(jnp.int32, (SUBLANES, LANES), 0)
    ninf = jnp.full((SUBLANES, LANES), -jnp.inf, F32)
    for lt in range(tq // LANES):
        ls = slice(lt * LANES, (lt + 1) * LANES)
        packed = []
        for p in range(2):
            pk = [ninf] * topk
            for h in range(heads):
                r0 = (2 * h + p) * nk
                v = [sc_ref[r0 + SUBLANES * g:r0 + SUBLANES * (g + 1), ls] for g in range(nk // SUBLANES)]
                v = _sort_desc(_pad_pow2(v, topk, ninf))[:topk]
                shift = SUBLANES // 2
                while shift >= 1:
                    v = _merge_top(v, [pltpu.roll(x, shift, 0) for x in v])
                    shift //= 2
                pk = [jnp.where(sub == h, v[a], pk[a]) for a in range(topk)]
            packed.append(pk)
        t1, t2 = packed
        lists = [[t1[a] + t2[b] for b in range(topk // (a + 1))] for a in range(topk)]
        best = lists[0]
        rest = [x for l in lists[2:] for x in l]
        if len(lists) > 1:
            best = _merge_top(best, lists[1] + [ninf] * (topk - len(lists[1])))
        if rest:
            best = _merge_top(best, _sort_desc(_pad_pow2(rest, topk, ninf))[:topk])
        m = t1[0] + t2[0]
        z = jnp.exp(best[0] - m)
        for a in range(1, topk):
            z = z + jnp.exp(best[a] - m)
        st_ref[0 * SUBLANES:1 * SUBLANES, ls] = best[topk - 1]
        st_ref[1 * SUBLANES:2 * SUBLANES, ls] = t1[0]
        st_ref[2 * SUBLANES:3 * SUBLANES, ls] = t2[0]
        st_ref[3 * SUBLANES:4 * SUBLANES, ls] = 1.0 / z


def _peer_scores(qt, keys):
    heads, _, nk, half = keys.shape
    t = qt.shape[1]
    tq = _pick(t, (256, 128))
    keys_f = keys.reshape(heads * 2, nk, half)
    assert heads <= SUBLANES and nk % SUBLANES == 0 and (PEER_TOPK & (PEER_TOPK - 1)) == 0
    return pl.pallas_call(
        functools.partial(_peer_score_kernel, heads=heads, nk=nk, half=half, topk=PEER_TOPK),
        grid=(t // tq,),
        in_specs=[pl.BlockSpec((heads * 2 * half, tq), lambda i: (0, i)),
                  pl.BlockSpec((heads * 2, nk, half), lambda i: (0, 0, 0))],
        out_specs=[pl.BlockSpec((heads * 2 * nk, tq), lambda i: (0, i)),
                   pl.BlockSpec((4 * SUBLANES, tq), lambda i: (0, i))],
        out_shape=[jax.ShapeDtypeStruct((heads * 2 * nk, t), F32),
                   jax.ShapeDtypeStruct((4 * SUBLANES, t), F32)],
        compiler_params=_params(("parallel",), 32 * 1024 * 1024),
        name="peer_scores",
    )(qt, keys_f)


def _peer_dense_kernel(xt_ref, u_ref, v_ref, sc_ref, st_ref, o_ref, e_ref, act_ref, p_ref, *, heads, nk, te):
    j = pl.program_id(1)
    tq = xt_ref.shape[1]
    nch = te // nk

    @pl.when(j == 0)
    def _():
        o_ref[...] = jnp.zeros_like(o_ref)
        for h in range(heads):
            m1 = st_ref[SUBLANES + h:SUBLANES + h + 1, :]
            m2 = st_ref[2 * SUBLANES + h:2 * SUBLANES + h + 1, :]
            rz = st_ref[3 * SUBLANES + h:3 * SUBLANES + h + 1, :]
            r1 = slice((2 * h) * nk, (2 * h + 1) * nk)
            r2 = slice((2 * h + 1) * nk, (2 * h + 2) * nk)
            e_ref[r1, :] = jnp.exp(sc_ref[r1, :] - m1)
            e_ref[r2, :] = jnp.exp(sc_ref[r2, :] - m2) * rz

    act_ref[...] = jnp.dot(u_ref[...], xt_ref[...], preferred_element_type=F32)

    s1_rows = [[sc_ref[pl.ds((2 * h) * nk + j * nch + il, 1), :] for h in range(heads)] for il in range(nch)]
    e1_rows = [[e_ref[pl.ds((2 * h) * nk + j * nch + il, 1), :] for h in range(heads)] for il in range(nch)]
    for cs in range(tq // LANES):
        lanes = slice(cs * LANES, (cs + 1) * LANES)
        tau = [st_ref[h:h + 1, lanes] for h in range(heads)]
        s1 = [[r[:, lanes] for r in rows] for rows in s1_rows]
        e1 = [[r[:, lanes] for r in rows] for rows in e1_rows]
        for rs in range(nk // PEER_STRIP):
            accs = [None] * nch
            for h in range(heads):
                r2 = slice((2 * h + 1) * nk + rs * PEER_STRIP, (2 * h + 1) * nk + (rs + 1) * PEER_STRIP)
                s2 = sc_ref[r2, lanes]
                e2 = e_ref[r2, lanes]
                for il in range(nch):
                    wh = e1[il][h] * jnp.where((s1[il][h] + s2) >= tau[h], e2, 0.0)
                    accs[il] = wh if accs[il] is None else accs[il] + wh
            for il in range(nch):
                rows = slice(il * nk + rs * PEER_STRIP, il * nk + (rs + 1) * PEER_STRIP)
                a = act_ref[rows, lanes]
                g = 0.5 * a * (1.0 + lax.erf(a * (2.0 ** -0.5)))
                p_ref[rows, lanes] = (accs[il] * g).astype(BF16)

    o_ref[...] += lax.dot_general(p_ref[...], v_ref[...], (((0,), (0,)), ((), ())), preferred_element_type=F32)


def _peer_dense(xt, u, v, sc, st, heads, nk):
    d, t = xt.shape
    ne = u.shape[0]
    tq = _pick(t, (512, 256, 128))
    te = _pick(ne, (512, 256, 128))
    assert te % nk == 0 and ne == nk * nk and nk % PEER_STRIP == 0
    vmem = (d * tq * 2 + 2 * heads * nk * tq * 4 + 2 * 4 * SUBLANES * tq * 4 + 2 * (2 * te * d * 2 + tq * d * 4)
            + 2 * heads * nk * tq * 4 + te * tq * 6 + 4 * te * tq * 4)
    once = pl.Buffered(1)
    return pl.pallas_call(
        functools.partial(_peer_dense_kernel, heads=heads, nk=nk, te=te),
        grid=(t // tq, ne // te),
        in_specs=[pl.BlockSpec((d, tq), lambda i, j: (0, i), pipeline_mode=once),
                  pl.BlockSpec((te, d), lambda i, j: (j, 0)),
                  pl.BlockSpec((te, d), lambda i, j: (j, 0)),
                  pl.BlockSpec((2 * heads * nk, tq), lambda i, j: (0, i), pipeline_mode=once),
                  pl.BlockSpec((4 * SUBLANES, tq), lambda i, j: (0, i))],
        out_specs=pl.BlockSpec((tq, d), lambda i, j: (i, 0)),
        out_shape=jax.ShapeDtypeStruct((t, d), F32),
        scratch_shapes=[pltpu.VMEM((2 * heads * nk, tq), F32),
                        pltpu.VMEM((te, tq), F32),
                        pltpu.VMEM((te, tq), BF16)],
        compiler_params=_params(("parallel", "arbitrary"), vmem),
        name="peer_dense",
    )(xt, u, v, sc, st)


def kernel(x, meta, ln0_g, ln0_b, w_in, w_a2, b_a, gla_norm_g, w_gla_o, conv_w, conv_b, conv_ln_g, conv_ln_b, w_conv_o, b_conv_o, w_out, ln1_g, ln1_b, peer_wq, peer_keys, peer_u, peer_v, ln2_g, ln2_b):
    bsz, seq, d = x.shape
    n_meta = meta.shape[0]
    depth = w_in.shape[0]
    kw = w_a2.shape[2]
    vw = w_gla_o.shape[1]
    rank = w_a2.shape[1]
    cc = conv_w.shape[2]
    front = _pick(seq, (ROW_ALIGN, 128, CHUNK))
    assert seq % front == 0 and front % CHUNK == 0 and 0 < n_meta <= CHUNK
    assert depth == 1, "zero pad rows are only maintained for a single layer"
    lp = -(-(front + seq) // ROW_ALIGN) * ROW_ALIGN
    alpha = (2.0 * depth) ** 0.25
    off_a = 2 * kw + 2 * vw
    off_c = off_a + rank
    assert off_a % LANES == 0 and rank <= LANES
    w_bf = w_in[0].astype(BF16)
    w_cg = w_in[0][:, off_c:].astype(BF16)

    hb = _ln0(x, meta, ln0_g, ln0_b, front, lp)
    qkvg = _matmul(hb, w_bf, BF16, name="proj_qkvg", col0=0, n=off_a)
    a_lr = _matmul(hb, w_bf, F32, name="proj_gate_lowrank", col0=off_a, n=LANES)
    o_gated = _gla(qkvg, a_lr, w_a2[0], b_a[0], gla_norm_g[0], bsz, lp)

    c = _glu_matmul(hb, w_cg, 0, cc)
    c_act = _conv(c, conv_w[0], conv_b[0], conv_ln_g[0], conv_ln_b[0])

    gates = _matmul(hb, w_cg, BF16, act="sigmoid", name="proj_branch_gates", col0=2 * cc, n=2 * d)
    merged = _merge(o_gated, c_act, w_gla_o[0].astype(BF16), w_conv_o[0].astype(BF16), b_conv_o[0], gates)
    mix = _matmul(merged, w_out[0].astype(BF16), F32, name="proj_out")
    h1, h1t = _res_ln(hb, mix, ln1_g[0], ln1_b[0], alpha)

    heads, _, nk, _ = peer_keys[0].shape
    qt = _matmul(peer_wq[0].T.astype(BF16), h1t, BF16, name="peer_query")
    sc, st = _peer_scores(qt, peer_keys[0].astype(BF16))
    ffn = _peer_dense(h1t, peer_u[0].astype(BF16), peer_v[0].astype(BF16), sc, st, heads, nk)
    return _final_res_ln(h1, ffn, ln2_g[0], ln2_b[0], alpha, bsz, front, lp, seq)
```

```python
import functools
import math

import jax
import jax.numpy as jnp
from jax import lax
from jax.experimental import pallas as pl
from jax.experimental.pallas import tpu as pltpu

F32 = jnp.float32
BF16 = jnp.bfloat16

CHUNK = 64
GLA_DK = 128
GLA_DV = 256
GATE_TAU = 16.0
PEER_TOPK = 16
LN_EPS = 1e-5
RMS_EPS = 1e-6
SUBLANES = 8
LANES = 128
PEER_STRIP = 64
CONV_ROWS = 128
ROW_ALIGN = 256
V7X_VMEM_BYTES = 64 * 1024 * 1024
VMEM_CAP = V7X_VMEM_BYTES - 8 * 1024 * 1024


def _pick(n, prefs):
    for p in prefs:
        if n % p == 0:
            return p
    return n


def _params(sem, vmem_bytes):
    return pltpu.CompilerParams(dimension_semantics=sem,
                                vmem_limit_bytes=int(min(max(vmem_bytes, 16 * 1024 * 1024), VMEM_CAP)))


def _ln_rows(x, g, b):
    mu = jnp.mean(x, axis=-1, keepdims=True)
    xc = x - mu
    var = jnp.mean(xc * xc, axis=-1, keepdims=True)
    return xc * lax.rsqrt(var + LN_EPS) * g + b


def _sigmoid(x):
    return 1.0 / (1.0 + jnp.exp(-x))


def _ln0_kernel(x_ref, meta_ref, g_ref, b_ref, h_ref, *, n_meta, n_xt):
    i = pl.program_id(1)
    g = g_ref[...]
    b = b_ref[...]
    front = h_ref.shape[1]

    @pl.when(i == 0)
    def _():
        pad = front - n_meta
        y = _ln_rows(meta_ref[...], g, b)
        h_ref[0, 0:pad, :] = jnp.zeros((pad, y.shape[1]), BF16)
        h_ref[0, pad:front, :] = y.astype(BF16)

    @pl.when(jnp.logical_and(i >= 1, i <= n_xt))
    def _():
        h_ref[0] = _ln_rows(x_ref[0], g, b).astype(BF16)

    @pl.when(i > n_xt)
    def _():
        h_ref[...] = jnp.zeros_like(h_ref)


def _ln0(x, meta, g, b, front, lp):
    bsz, seq, d = x.shape
    n_meta = meta.shape[0]
    n_xt = seq // front
    h = pl.pallas_call(
        functools.partial(_ln0_kernel, n_meta=n_meta, n_xt=n_xt),
        grid=(bsz, lp // front),
        in_specs=[
            pl.BlockSpec((1, front, d), lambda bb, i: (bb, jnp.clip(i - 1, 0, n_xt - 1), 0)),
            pl.BlockSpec((n_meta, d), lambda bb, i: (0, 0)),
            pl.BlockSpec((1, d), lambda bb, i: (0, 0)),
            pl.BlockSpec((1, d), lambda bb, i: (0, 0)),
        ],
        out_specs=pl.BlockSpec((1, front, d), lambda bb, i: (bb, i, 0)),
        out_shape=jax.ShapeDtypeStruct((bsz, lp, d), BF16),
        compiler_params=_params(("parallel", "arbitrary"), 24 * front * d),
        name="ln0",
    )(x, meta, g.reshape(1, d), b.reshape(1, d))
    return h.reshape(bsz * lp, d)


def _mm_kernel(a_ref, b_ref, o_ref, *, act):
    acc = jnp.dot(a_ref[...], b_ref[...], preferred_element_type=F32)
    if act == "sigmoid":
        acc = _sigmoid(acc)
    o_ref[...] = acc.astype(o_ref.dtype)


def _matmul(a, b, out_dtype, *, act=None, name, col0=0, n=None):
    m, k = a.shape
    n = b.shape[1] if n is None else n
    tm = _pick(m, (768, 512, 256, 128))
    tn = _pick(math.gcd(n, col0) if col0 else n, (1024, 512, 256, 128))
    assert n % tn == 0 and col0 % tn == 0
    cb = col0 // tn
    osz = jnp.dtype(out_dtype).itemsize
    vmem = 2 * (tm * k * 2 + k * tn * 2 + tm * tn * osz) + 5 * tm * tn * 4
    return pl.pallas_call(
        functools.partial(_mm_kernel, act=act),
        grid=(m // tm, n // tn),
        in_specs=[pl.BlockSpec((tm, k), lambda i, j: (i, 0)),
                  pl.BlockSpec((k, tn), lambda i, j: (0, j + cb))],
        out_specs=pl.BlockSpec((tm, tn), lambda i, j: (i, j)),
        out_shape=jax.ShapeDtypeStruct((m, n), out_dtype),
        compiler_params=_params(("parallel", "arbitrary"), vmem),
        name=name,
    )(a, b)


def _mm_wstat_kernel(a_ref, w_ref, o_ref, wb_ref):
    @pl.when(pl.program_id(1) == 0)
    def _():
        wb_ref[...] = w_ref[...].astype(BF16)

    o_ref[...] = jnp.dot(a_ref[...], wb_ref[...], preferred_element_type=F32).astype(o_ref.dtype)


def _matmul_wstat(a, w, n, out_dtype, *, name):
    m, k = a.shape
    tm = _pick(m, (768, 512, 256, 128))
    tn = _pick(n, (512, 256, 128))
    osz = jnp.dtype(out_dtype).itemsize
    vmem = 2 * (tm * k * 2 + k * tn * 4 + tm * tn * osz) + k * tn * 2 + 5 * tm * tn * 4 + k * tn * 4
    return pl.pallas_call(
        _mm_wstat_kernel,
        grid=(n // tn, m // tm),
        in_specs=[pl.BlockSpec((tm, k), lambda j, i: (i, 0)),
                  pl.BlockSpec((k, tn), lambda j, i: (0, j))],
        out_specs=pl.BlockSpec((tm, tn), lambda j, i: (i, j)),
        out_shape=jax.ShapeDtypeStruct((m, n), out_dtype),
        scratch_shapes=[pltpu.VMEM((k, tn), BF16)],
        compiler_params=_params(("parallel", "arbitrary"), vmem),
        name=name,
    )(a, w)


def _glu_kernel(a_ref, b1_ref, b2_ref, o_ref):
    a = a_ref[...]
    d1 = jnp.dot(a, b1_ref[...], preferred_element_type=F32)
    d2 = jnp.dot(a, b2_ref[...], preferred_element_type=F32)
    o_ref[...] = (d1 * _sigmoid(d2)).astype(o_ref.dtype)


def _glu_matmul(a, b, col0, n):
    m, k = a.shape
    tm = _pick(m, (768, 512, 256, 128))
    tn = _pick(math.gcd(n, col0) if col0 else n, (512, 256, 128))
    assert n % tn == 0 and col0 % tn == 0
    nb = n // tn
    cb = col0 // tn
    vmem = 2 * (tm * k * 2 + 2 * k * tn * 2 + tm * tn * 2) + 8 * tm * tn * 4
    return pl.pallas_call(
        _glu_kernel,
        grid=(m // tm, nb),
        in_specs=[pl.BlockSpec((tm, k), lambda i, j: (i, 0)),
                  pl.BlockSpec((k, tn), lambda i, j: (0, j + cb)),
                  pl.BlockSpec((k, tn), lambda i, j: (0, j + cb + nb))],
        out_specs=pl.BlockSpec((tm, tn), lambda i, j: (i, j)),
        out_shape=jax.ShapeDtypeStruct((m, n), BF16),
        compiler_params=_params(("parallel", "arbitrary"), vmem),
        name="glu_proj",
    )(a, b, b)


def _gla_kernel(q_ref, k_ref, v_ref, g_ref, a_ref, wa2_ref, ba_ref, gn_ref, o_ref, st_ref, *, heads, rank):
    c = pl.program_id(1)

    @pl.when(c == 0)
    def _():
        st_ref[...] = jnp.zeros_like(st_ref)

    a_lr = a_ref[:, 0:rank]
    z = jnp.dot(a_lr, wa2_ref[...], preferred_element_type=F32,
                precision=lax.Precision.HIGHEST) + ba_ref[...]
    log_a = jax.nn.log_sigmoid(z) / GATE_TAU
    row = lax.broadcasted_iota(jnp.int32, (CHUNK, CHUNK), 0)
    col = lax.broadcasted_iota(jnp.int32, (CHUNK, CHUNK), 1)
    tri = (row >= col).astype(F32)
    cum = jnp.dot(tri, log_a, preferred_element_type=F32, precision=lax.Precision.HIGHEST)
    tot = cum[CHUNK - 1:CHUNK, :]
    k_scale = jnp.exp(tot - cum)
    decay = jnp.exp(tot)
    gn = gn_ref[...]
    q_scale = GLA_DK ** -0.5
    for h in range(heads):
        sk = slice(h * GLA_DK, (h + 1) * GLA_DK)
        sv = slice(h * GLA_DV, (h + 1) * GLA_DV)
        kd = (k_ref[:, sk].astype(F32) * k_scale[:, sk]).astype(BF16)
        upd = lax.dot_general(v_ref[:, sv], kd, (((0,), (0,)), ((), ())), preferred_element_type=F32)
        st = decay[:, sk] * st_ref[h] + upd
        st_ref[h] = st
        q = (q_ref[:, sk].astype(F32) * q_scale).astype(BF16)
        o = lax.dot_general(q, st.astype(BF16), (((1,), (1,)), ((), ())), preferred_element_type=F32)
        y = o * lax.rsqrt(jnp.mean(o * o, axis=-1, keepdims=True) + RMS_EPS) * gn
        gg = g_ref[:, sv].astype(F32)
        o_ref[:, sv] = (y * (gg * _sigmoid(gg))).astype(o_ref.dtype)


def _gla(qkvg, a_lr, w_a2, b_a, gn, bsz, lp):
    t, _ = qkvg.shape
    kw = w_a2.shape[1]
    heads = kw // GLA_DK
    vw = heads * GLA_DV
    rank = w_a2.shape[0]
    nc = lp // CHUNK
    aw = a_lr.shape[1]
    assert kw == vw // 2
    row = lambda bb, c: bb * nc + c
    vmem = 2 * (2 * CHUNK * kw * 2 + 3 * CHUNK * vw * 2) + heads * GLA_DK * GLA_DV * 4 + 12 * CHUNK * kw * 4
    return pl.pallas_call(
        functools.partial(_gla_kernel, heads=heads, rank=rank),
        grid=(bsz, nc),
        in_specs=[
            pl.BlockSpec((CHUNK, kw), lambda bb, c: (row(bb, c), 0)),
            pl.BlockSpec((CHUNK, kw), lambda bb, c: (row(bb, c), 1)),
            pl.BlockSpec((CHUNK, vw), lambda bb, c: (row(bb, c), 1)),
            pl.BlockSpec((CHUNK, vw), lambda bb, c: (row(bb, c), 2)),
            pl.BlockSpec((CHUNK, aw), lambda bb, c: (row(bb, c), 0)),
            pl.BlockSpec((rank, kw), lambda bb, c: (0, 0)),
            pl.BlockSpec((1, kw), lambda bb, c: (0, 0)),
            pl.BlockSpec((1, GLA_DV), lambda bb, c: (0, 0)),
        ],
        out_specs=pl.BlockSpec((CHUNK, vw), lambda bb, c: (row(bb, c), 0)),
        out_shape=jax.ShapeDtypeStruct((t, vw), BF16),
        scratch_shapes=[pltpu.VMEM((heads, GLA_DV, GLA_DK), F32)],
        compiler_params=_params(("parallel", "arbitrary"), vmem),
        name="gla",
    )(qkvg, qkvg, qkvg, qkvg, a_lr, w_a2, b_a.reshape(1, kw), gn.reshape(1, GLA_DV))


def _conv_kernel(c_ref, halo_ref, w_ref, b_ref, lg_ref, lb_ref, o_ref, buf_ref, y_ref, *, tc, hb, width):
    i = pl.program_id(0)
    d = c_ref.shape[1]
    halo = halo_ref[...].astype(F32)
    buf_ref[0:hb, :] = jnp.where(i == 0, 0.0, halo)
    buf_ref[hb:hb + tc, :] = c_ref[...].astype(F32)
    buf_ref[hb + tc:hb + tc + SUBLANES, :] = jnp.zeros((SUBLANES, d), F32)
    base = hb - (width - 1)
    rows = CONV_ROWS
    n_rc = tc // rows

    def chunk(t, carry):
        r0 = pl.multiple_of((t % n_rc) * rows, rows)
        l0 = pl.multiple_of((t // n_rc) * LANES, LANES)
        acc = jnp.zeros((rows, LANES), F32) + b_ref[:, pl.ds(l0, LANES)]
        for s in range(SUBLANES):
            part = None
            for j in range(width):
                if (base + j) % SUBLANES != s:
                    continue
                a0 = ((base + j) // SUBLANES) * SUBLANES
                term = w_ref[j:j + 1, pl.ds(l0, LANES)] * buf_ref[pl.ds(r0 + a0, rows + SUBLANES), pl.ds(l0, LANES)]
                part = term if part is None else part + term
            if part is not None:
                acc = acc + part[s:s + rows, :]
        y_ref[pl.ds(r0, rows), pl.ds(l0, LANES)] = acc
        return carry

    lax.fori_loop(0, n_rc * (d // LANES), chunk, 0)
    y = _ln_rows(y_ref[...], lg_ref[...], lb_ref[...])
    o_ref[...] = (y * _sigmoid(y)).astype(o_ref.dtype)


def _conv(c, w, b, lg, lb):
    t, d = c.shape
    width = w.shape[0]
    hb = 32
    tc = _pick(t, (256, 128))
    assert width - 1 <= hb <= CHUNK and tc % CONV_ROWS == 0 and d % LANES == 0
    r = tc // hb
    vmem = 2 * (tc * d * 2 * 2 + hb * d * 2) + (2 * tc + hb + SUBLANES) * d * 4 + 6 * tc * d * 4
    return pl.pallas_call(
        functools.partial(_conv_kernel, tc=tc, hb=hb, width=width),
        grid=(t // tc,),
        in_specs=[
            pl.BlockSpec((tc, d), lambda i: (i, 0)),
            pl.BlockSpec((hb, d), lambda i: (jnp.maximum(i * r - 1, 0), 0)),
            pl.BlockSpec((width, d), lambda i: (0, 0)),
            pl.BlockSpec((1, d), lambda i: (0, 0)),
            pl.BlockSpec((1, d), lambda i: (0, 0)),
            pl.BlockSpec((1, d), lambda i: (0, 0)),
        ],
        out_specs=pl.BlockSpec((tc, d), lambda i: (i, 0)),
        out_shape=jax.ShapeDtypeStruct((t, d), BF16),
        scratch_shapes=[pltpu.VMEM((tc + hb + SUBLANES, d), F32), pltpu.VMEM((tc, d), F32)],
        compiler_params=_params(("parallel",), vmem),
        name="conv_ln_silu",
    )(c, c, w, b.reshape(1, d), lg.reshape(1, d), lb.reshape(1, d))


def _merge_kernel(o_ref_in, c_ref, wg_ref, wc_ref, bc_ref, gg_ref, gc_ref, out_ref):
    yg = jnp.dot(o_ref_in[...], wg_ref[...], preferred_element_type=F32)
    yc = jnp.dot(c_ref[...], wc_ref[...], preferred_element_type=F32) + bc_ref[...]
    out = gg_ref[...].astype(F32) * yg + gc_ref[...].astype(F32) * yc
    out_ref[...] = out.astype(out_ref.dtype)


def _merge(o_gated, c_act, w_gla_o, w_conv_o, b_conv_o, gates):
    m, k1 = o_gated.shape
    k2 = c_act.shape[1]
    n = w_gla_o.shape[1]
    tm = _pick(m, (768, 512, 256, 128))
    tn = _pick(n, (512, 256, 128))
    nb = n // tn
    vmem = 2 * (tm * (k1 + k2) * 2 + (k1 + k2) * tn * 2 + 3 * tm * tn * 2) + 4 * tm * tn * 4
    return pl.pallas_call(
        _merge_kernel,
        grid=(m // tm, nb),
        in_specs=[
            pl.BlockSpec((tm, k1), lambda i, j: (i, 0)),
            pl.BlockSpec((tm, k2), lambda i, j: (i, 0)),
            pl.BlockSpec((k1, tn), lambda i, j: (0, j)),
            pl.BlockSpec((k2, tn), lambda i, j: (0, j)),
            pl.BlockSpec((1, tn), lambda i, j: (0, j)),
            pl.BlockSpec((tm, tn), lambda i, j: (i, j)),
            pl.BlockSpec((tm, tn), lambda i, j: (i, j + nb)),
        ],
        out_specs=pl.BlockSpec((tm, tn), lambda i, j: (i, j)),
        out_shape=jax.ShapeDtypeStruct((m, n), BF16),
        compiler_params=_params(("parallel", "arbitrary"), vmem),
        name="branch_merge",
    )(o_gated, c_act, w_gla_o, w_conv_o, b_conv_o.reshape(1, n), gates, gates)


def _resln_kernel(h_ref, y_ref, g_ref, b_ref, of_ref, ot_ref, *, alpha):
    y = _ln_rows(alpha * h_ref[...].astype(F32) + y_ref[...], g_ref[...], b_ref[...])
    of_ref[...] = y
    ot_ref[...] = y.T.astype(BF16)


def _res_ln(h, y, g, b, alpha):
    t, d = h.shape
    tr = _pick(t, (256, 128))
    out_specs = [pl.BlockSpec((tr, d), lambda i: (i, 0)), pl.BlockSpec((d, tr), lambda i: (0, i))]
    out_shape = [jax.ShapeDtypeStruct((t, d), F32), jax.ShapeDtypeStruct((d, t), BF16)]
    return pl.pallas_call(
        functools.partial(_resln_kernel, alpha=alpha),
        grid=(t // tr,),
        in_specs=[pl.BlockSpec((tr, d), lambda i: (i, 0)),
                  pl.BlockSpec((tr, d), lambda i: (i, 0)),
                  pl.BlockSpec((1, d), lambda i: (0, 0)),
                  pl.BlockSpec((1, d), lambda i: (0, 0))],
        out_specs=out_specs,
        out_shape=out_shape,
        compiler_params=_params(("parallel",), 16 * tr * d * 4),
        name="res_ln_t",
    )(h, y, g.reshape(1, d), b.reshape(1, d))


def _final_ln_kernel(h_ref, y_ref, g_ref, b_ref, o_ref, *, alpha):
    o_ref[0] = _ln_rows(alpha * h_ref[...] + y_ref[...], g_ref[...], b_ref[...])


def _final_res_ln(h, y, g, b, alpha, bsz, front, lp, seq):
    d = h.shape[1]
    nt = lp // front
    row = lambda bb, c: (bb * nt + c + 1, 0)
    return pl.pallas_call(
        functools.partial(_final_ln_kernel, alpha=alpha),
        grid=(bsz, seq // front),
        in_specs=[pl.BlockSpec((front, d), row),
                  pl.BlockSpec((front, d), row),
                  pl.BlockSpec((1, d), lambda bb, c: (0, 0)),
                  pl.BlockSpec((1, d), lambda bb, c: (0, 0))],
        out_specs=pl.BlockSpec((1, front, d), lambda bb, c: (bb, c, 0)),
        out_shape=jax.ShapeDtypeStruct((bsz, seq, d), F32),
        compiler_params=_params(("parallel", "parallel"), 16 * front * d * 4),
        name="final_res_ln",
    )(h, y, g.reshape(1, d), b.reshape(1, d))


def _sort_desc(v):
    n = len(v)
    k = 2
    while k <= n:
        j = k // 2
        while j >= 1:
            for i in range(n):
                l = i ^ j
                if l > i:
                    hi, lo = jnp.maximum(v[i], v[l]), jnp.minimum(v[i], v[l])
                    v[i], v[l] = (hi, lo) if (i & k) == 0 else (lo, hi)
            j //= 2
        k *= 2
    return v


def _merge_desc(v):
    n = len(v)
    j = n // 2
    while j >= 1:
        for i in range(n):
            l = i ^ j
            if l > i:
                v[i], v[l] = jnp.maximum(v[i], v[l]), jnp.minimum(v[i], v[l])
        j //= 2
    return v


def _merge_top(a, b):
    n = len(a)
    return _merge_desc([jnp.maximum(a[i], b[n - 1 - i]) for i in range(n)])


def _pad_pow2(v, n, fill):
    m = max(n, 1 << (len(v) - 1).bit_length())
    return v + [fill] * (m - len(v))


def _peer_score_kernel(qt_ref, keys_ref, sc_ref, st_ref, *, heads, nk, half, topk):
    tq = qt_ref.shape[1]
    for idx in range(2 * heads):
        sc_ref[idx * nk:(idx + 1) * nk, :] = jnp.dot(
            keys_ref[idx], qt_ref[idx * half:(idx + 1) * half, :], preferred_element_type=F32)
    sub = lax.broadcasted_iota(jnp.int32, (SUBLANES, LANES), 0)
    ninf = jnp.full((SUBLANES, LANES), -jnp.inf, F32)
    for lt in range(tq // LANES):
        ls = slice(lt * LANES, (lt + 1) * LANES)
        packed = []
        for p in range(2):
            pk = [ninf] * topk
            for h in range(heads):
                r0 = (2 * h + p) * nk
                v = [sc_ref[r0 + SUBLANES * g:r0 + SUBLANES * (g + 1), ls] for g in range(nk // SUBLANES)]
                v = _sort_desc(_pad_pow2(v, topk, ninf))[:topk]
                shift = SUBLANES // 2
                while shift >= 1:
                    v = _merge_top(v, [pltpu.roll(x, shift, 0) for x in v])
                    shift //= 2
                pk = [jnp.where(sub == h, v[a], pk[a]) for a in range(topk)]
            packed.append(pk)
        t1, t2 = packed
        lists = [[t1[a] + t2[b] for b in range(topk // (a + 1))] for a in range(topk)]
        best = lists[0]
        rest = [x for l in lists[2:] for x in l]
        if len(lists) > 1:
            best = _merge_top(best, lists[1] + [ninf] * (topk - len(lists[1])))
        if rest:
            best = _merge_top(best, _sort_desc(_pad_pow2(rest, topk, ninf))[:topk])
        m = t1[0] + t2[0]
        z = jnp.exp(best[0] - m)
        for a in range(1, topk):
            z = z + jnp.exp(best[a] - m)
        st_ref[0 * SUBLANES:1 * SUBLANES, ls] = best[topk - 1]
        st_ref[1 * SUBLANES:2 * SUBLANES, ls] = t1[0]
        st_ref[2 * SUBLANES:3 * SUBLANES, ls] = t2[0]
        st_ref[3 * SUBLANES:4 * SUBLANES, ls] = 1.0 / z


def _peer_scores(qt, keys):
    heads, _, nk, half = keys.shape
    t = qt.shape[1]
    tq = _pick(t, (256, 128))
    keys_f = keys.reshape(heads * 2, nk, half)
    assert heads <= SUBLANES and nk % SUBLANES == 0 and (PEER_TOPK & (PEER_TOPK - 1)) == 0
    return pl.pallas_call(
        functools.partial(_peer_score_kernel, heads=heads, nk=nk, half=half, topk=PEER_TOPK),
        grid=(t // tq,),
        in_specs=[pl.BlockSpec((heads * 2 * half, tq), lambda i: (0, i)),
                  pl.BlockSpec((heads * 2, nk, half), lambda i: (0, 0, 0))],
        out_specs=[pl.BlockSpec((heads * 2 * nk, tq), lambda i: (0, i)),
                   pl.BlockSpec((4 * SUBLANES, tq), lambda i: (0, i))],
        out_shape=[jax.ShapeDtypeStruct((heads * 2 * nk, t), F32),
                   jax.ShapeDtypeStruct((4 * SUBLANES, t), F32)],
        compiler_params=_params(("parallel",), 32 * 1024 * 1024),
        name="peer_scores",
    )(qt, keys_f)


def _peer_dense_kernel(xt_ref, u_ref, v_ref, sc_ref, st_ref, o_ref, e_ref, act_ref, p_ref, *, heads, nk, te):
    j = pl.program_id(1)
    tq = xt_ref.shape[1]
    nch = te // nk

    @pl.when(j == 0)
    def _():
        o_ref[...] = jnp.zeros_like(o_ref)
        for h in range(heads):
            m1 = st_ref[SUBLANES + h:SUBLANES + h + 1, :]
            m2 = st_ref[2 * SUBLANES + h:2 * SUBLANES + h + 1, :]
            rz = st_ref[3 * SUBLANES + h:3 * SUBLANES + h + 1, :]
            r1 = slice((2 * h) * nk, (2 * h + 1) * nk)
            r2 = slice((2 * h + 1) * nk, (2 * h + 2) * nk)
            e_ref[r1, :] = jnp.exp(sc_ref[r1, :] - m1)
            e_ref[r2, :] = jnp.exp(sc_ref[r2, :] - m2) * rz

    act_ref[...] = jnp.dot(u_ref[...], xt_ref[...], preferred_element_type=F32)

    s1_rows = [[sc_ref[pl.ds((2 * h) * nk + j * nch + il, 1), :] for h in range(heads)] for il in range(nch)]
    e1_rows = [[e_ref[pl.ds((2 * h) * nk + j * nch + il, 1), :] for h in range(heads)] for il in range(nch)]
    for cs in range(tq // LANES):
        lanes = slice(cs * LANES, (cs + 1) * LANES)
        tau = [st_ref[h:h + 1, lanes] for h in range(heads)]
        s1 = [[r[:, lanes] for r in rows] for rows in s1_rows]
        e1 = [[r[:, lanes] for r in rows] for rows in e1_rows]
        for rs in range(nk // PEER_STRIP):
            accs = [None] * nch
            for h in range(heads):
                r2 = slice((2 * h + 1) * nk + rs * PEER_STRIP, (2 * h + 1) * nk + (rs + 1) * PEER_STRIP)
                s2 = sc_ref[r2, lanes]
                e2 = e_ref[r2, lanes]
                for il in range(nch):
                    wh = e1[il][h] * jnp.where((s1[il][h] + s2) >= tau[h], e2, 0.0)
                    accs[il] = wh if accs[il] is None else accs[il] + wh
            for il in range(nch):
                rows = slice(il * nk + rs * PEER_STRIP, il * nk + (rs + 1) * PEER_STRIP)
                a = act_ref[rows, lanes]
                g = 0.5 * a * (1.0 + lax.erf(a * (2.0 ** -0.5)))
                p_ref[rows, lanes] = (accs[il] * g).astype(BF16)

    o_ref[...] += lax.dot_general(p_ref[...], v_ref[...], (((0,), (0,)), ((), ())), preferred_element_type=F32)


def _peer_dense(xt, u, v, sc, st, heads, nk):
    d, t = xt.shape
    ne = u.shape[0]
    tq = _pick(t, (512, 256, 128))
    te = _pick(ne, (512, 256, 128))
    assert te % nk == 0 and ne == nk * nk and nk % PEER_STRIP == 0
    vmem = (d * tq * 2 + 2 * heads * nk * tq * 4 + 2 * 4 * SUBLANES * tq * 4 + 2 * (2 * te * d * 2 + tq * d * 4)
            + 2 * heads * nk * tq * 4 + te * tq * 6 + 4 * te * tq * 4)
    once = pl.Buffered(1)
    return pl.pallas_call(
        functools.partial(_peer_dense_kernel, heads=heads, nk=nk, te=te),
        grid=(t // tq, ne // te),
        in_specs=[pl.BlockSpec((d, tq), lambda i, j: (0, i), pipeline_mode=once),
                  pl.BlockSpec((te, d), lambda i, j: (j, 0)),
                  pl.BlockSpec((te, d), lambda i, j: (j, 0)),
                  pl.BlockSpec((2 * heads * nk, tq), lambda i, j: (0, i), pipeline_mode=once),
                  pl.BlockSpec((4 * SUBLANES, tq), lambda i, j: (0, i))],
        out_specs=pl.BlockSpec((tq, d), lambda i, j: (i, 0)),
        out_shape=jax.ShapeDtypeStruct((t, d), F32),
        scratch_shapes=[pltpu.VMEM((2 * heads * nk, tq), F32),
                        pltpu.VMEM((te, tq), F32),
                        pltpu.VMEM((te, tq), BF16)],
        compiler_params=_params(("parallel", "arbitrary"), vmem),
        name="peer_dense",
    )(xt, u, v, sc, st)


def kernel(x, meta, ln0_g, ln0_b, w_in, w_a2, b_a, gla_norm_g, w_gla_o, conv_w, conv_b, conv_ln_g, conv_ln_b, w_conv_o, b_conv_o, w_out, ln1_g, ln1_b, peer_wq, peer_keys, peer_u, peer_v, ln2_g, ln2_b):
    bsz, seq, d = x.shape
    n_meta = meta.shape[0]
    depth = w_in.shape[0]
    kw = w_a2.shape[2]
    vw = w_gla_o.shape[1]
    rank = w_a2.shape[1]
    cc = conv_w.shape[2]
    front = _pick(seq, (ROW_ALIGN, 128, CHUNK))
    assert seq % front == 0 and front % CHUNK == 0 and 0 < n_meta <= CHUNK
    assert depth == 1, "zero pad rows are only maintained for a single layer"
    lp = -(-(front + seq) // ROW_ALIGN) * ROW_ALIGN
    alpha = (2.0 * depth) ** 0.25
    off_a = 2 * kw + 2 * vw
    off_c = off_a + rank
    assert off_a % LANES == 0 and rank <= LANES
    w_a = w_in[0][:, off_a:off_a + LANES].astype(BF16)
    w_cg = w_in[0][:, off_c:].astype(BF16)

    hb = _ln0(x, meta, ln0_g, ln0_b, front, lp)
    qkvg = _matmul_wstat(hb, w_in[0], off_a, BF16, name="proj_qkvg")
    a_lr = _matmul(hb, w_a, F32, name="proj_gate_lowrank")
    o_gated = _gla(qkvg, a_lr, w_a2[0], b_a[0], gla_norm_g[0], bsz, lp)

    c = _glu_matmul(hb, w_cg, 0, cc)
    c_act = _conv(c, conv_w[0], conv_b[0], conv_ln_g[0], conv_ln_b[0])

    gates = _matmul(hb, w_cg, BF16, act="sigmoid", name="proj_branch_gates", col0=2 * cc, n=2 * d)
    merged = _merge(o_gated, c_act, w_gla_o[0].astype(BF16), w_conv_o[0].astype(BF16), b_conv_o[0], gates)
    mix = _matmul(merged, w_out[0].astype(BF16), F32, name="proj_out")
    h1, h1t = _res_ln(hb, mix, ln1_g[0], ln1_b[0], alpha)

    heads, _, nk, _ = peer_keys[0].shape
    qt = _matmul(peer_wq[0].T.astype(BF16), h1t, BF16, name="peer_query")
    sc, st = _peer_scores(qt, peer_keys[0].astype(BF16))
    ffn = _peer_dense(h1t, peer_u[0].astype(BF16), peer_v[0].astype(BF16), sc, st, heads, nk)
    return _final_res_ln(h1, ffn, ln2_g[0], ln2_b[0], alpha, bsz, front, lp, seq)
```
